```python
import math
import jax, jax.numpy as jnp
from jax import lax
import numpy as np

D_MODEL = 1024
BATCH = 8
SEQ = 2048
DEPTH = 2

HEAD_DIM = 64
MEM_HEADS = 4
MEM_WIDTH = MEM_HEADS * HEAD_DIM
SB_HEADS = (D_MODEL - MEM_WIDTH) // (2 * HEAD_DIM)
MOBA_HEADS = SB_HEADS
SB_WIDTH = SB_HEADS * HEAD_DIM
MOBA_WIDTH = MOBA_HEADS * HEAD_DIM
MIX_WIDTH = SB_WIDTH + MOBA_WIDTH + MEM_WIDTH
IN_COLS = 4 * SB_WIDTH + 4 * MOBA_WIDTH + 2 * MEM_WIDTH
MEM_LEN = 256
SB_BLOCK = 128
MOBA_BLOCK = 256
MOBA_TOPK = 3
MOBA_Q_CHUNK = 64
ROPE_THETA = 500000.0
ROPE_DIMS = HEAD_DIM // 4
NORM_EPS = 1e-6
NEG_INF = -1e30

kernel_name = "hybrid_stickbreak_moba_memxattn"


def rms_norm(x, g):
    xf = x.astype(jnp.float32)
    y = xf * lax.rsqrt(jnp.mean(xf * xf, axis=-1, keepdims=True) + NORM_EPS)
    return (y * g.astype(jnp.float32)).astype(x.dtype)


def split_heads(t, n_heads):
    b, s, _ = t.shape
    return t.reshape(b, s, n_heads, HEAD_DIM).transpose(0, 2, 1, 3)


def merge_heads(t):
    b, h, s, d = t.shape
    return t.transpose(0, 2, 1, 3).reshape(b, s, h * d)


def partial_rotary(t, pos):
    half = ROPE_DIMS // 2
    inv_freq = jnp.float32(ROPE_THETA) ** (-jnp.arange(half, dtype=jnp.float32) * 2.0 / ROPE_DIMS)
    ang = pos.astype(jnp.float32)[:, None] * inv_freq[None, :]
    cos, sin = jnp.cos(ang), jnp.sin(ang)
    tf = t.astype(jnp.float32)
    x1, x2, rest = tf[..., :half], tf[..., half:ROPE_DIMS], tf[..., ROPE_DIMS:]
    out = jnp.concatenate([x1 * cos - x2 * sin, x2 * cos + x1 * sin, rest], axis=-1)
    return out.astype(t.dtype)


def stick_breaking_attention(q, k, v):
    T = q.shape[2]
    scale = HEAD_DIM ** -0.5
    outs = []
    for i in range(T // SB_BLOCK):
        q0, q1 = i * SB_BLOCK, (i + 1) * SB_BLOCK
        qb = q[:, :, q0:q1]
        kb, vb = k[:, :, :q1], v[:, :, :q1]
        z = jnp.einsum('bhqd,bhkd->bhqk', qb, kb, preferred_element_type=jnp.float32) * scale
        past = jnp.arange(q1)[None, :] < jnp.arange(q0, q1)[:, None]
        log_beta = jax.nn.log_sigmoid(z)
        log_one_minus = jnp.where(past, log_beta - z, 0.0)
        later = lax.cumsum(log_one_minus, axis=3, reverse=True) - log_one_minus
        w = jnp.where(past, jnp.exp(log_beta + later), 0.0)
        outs.append(jnp.einsum('bhqk,bhkd->bhqd', w.astype(v.dtype), vb))
    return jnp.concatenate(outs, axis=2)


def moba_attention(q, k, v):
    B, H, T, dh = q.shape
    n_blk = -(-T // MOBA_BLOCK)
    Tp = n_blk * MOBA_BLOCK
    pad = [(0, 0), (0, 0), (0, Tp - T), (0, 0)]
    kp, vp = jnp.pad(k, pad), jnp.pad(v, pad)
    kblk = kp.reshape(B, H, n_blk, MOBA_BLOCK, dh)
    vblk = vp.reshape(B, H, n_blk, MOBA_BLOCK, dh)
    kmean = jnp.mean(kblk.astype(jnp.float32), axis=3)
    topk = min(MOBA_TOPK, n_blk - 1)
    scale = dh ** -0.5
    bi = jnp.arange(B)[:, None, None, None]
    hi = jnp.arange(H)[None, :, None, None]

    def chunk(ci):
        c0 = ci * MOBA_Q_CHUNK
        qc = lax.dynamic_slice_in_dim(q, c0, MOBA_Q_CHUNK, axis=2)
        t_pos = c0 + jnp.arange(MOBA_Q_CHUNK)
        own = c0 // MOBA_BLOCK
        own_start = own * MOBA_BLOCK
        k_own = lax.dynamic_slice_in_dim(kp, own_start, MOBA_BLOCK, axis=2)
        v_own = lax.dynamic_slice_in_dim(vp, own_start, MOBA_BLOCK, axis=2)
        s_own = jnp.einsum('bhqd,bhkd->bhqk', qc, k_own, preferred_element_type=jnp.float32) * scale
        causal = (own_start + jnp.arange(MOBA_BLOCK))[None, :] <= t_pos[:, None]
        s_own = jnp.where(causal, s_own, NEG_INF)
        if topk > 0:
            gate = jnp.einsum('bhqd,bhnd->bhqn', qc.astype(jnp.float32), kmean)
            gate = jnp.where(jnp.arange(n_blk) < own, gate, -jnp.inf)
            _, sel = lax.top_k(gate, topk)
            sel_valid = sel < own
            k_sel = kblk[bi, hi, sel]
            v_sel = vblk[bi, hi, sel]
            s_sel = jnp.einsum('bhqd,bhqnkd->bhqnk', qc, k_sel, preferred_element_type=jnp.float32) * scale
            s_sel = jnp.where(sel_valid[..., None], s_sel, NEG_INF)
            s_sel = s_sel.reshape(B, H, MOBA_Q_CHUNK, topk * MOBA_BLOCK)
            p = jax.nn.softmax(jnp.concatenate([s_sel, s_own], axis=-1), axis=-1)
            p_sel = p[..., :topk * MOBA_BLOCK].reshape(B, H, MOBA_Q_CHUNK, topk, MOBA_BLOCK)
            p_own = p[..., topk * MOBA_BLOCK:]
            o = (jnp.einsum('bhqnk,bhqnkd->bhqd', p_sel.astype(v.dtype), v_sel)
                 + jnp.einsum('bhqk,bhkd->bhqd', p_own.astype(v.dtype), v_own))
        else:
            p_own = jax.nn.softmax(s_own, axis=-1)
            o = jnp.einsum('bhqk,bhkd->bhqd', p_own.astype(v.dtype), v_own)
        return o

    outs = lax.map(chunk, jnp.arange(T // MOBA_Q_CHUNK))
    return outs.transpose(1, 2, 0, 3, 4).reshape(B, H, T, dh)


def memory_attention(q, mk, mv):
    s = jnp.einsum('bhqd,bhmd->bhqm', q, mk, preferred_element_type=jnp.float32) * (HEAD_DIM ** -0.5)
    p = jax.nn.softmax(s, axis=-1)
    return jnp.einsum('bhqm,bhmd->bhqd', p.astype(mv.dtype), mv)


def setup_inputs(seed: int = 0) -> dict:
    key = jax.random.key(seed)
    ks = jax.random.split(key, 8)
    x = jax.random.normal(ks[0], (BATCH, SEQ, D_MODEL), jnp.float32)
    mem = jax.random.normal(ks[1], (BATCH, MEM_LEN, D_MODEL), jnp.float32)
    norm_g = 1.0 + 0.02 * jax.random.normal(ks[2], (DEPTH, D_MODEL), jnp.float32)
    w_in = jax.random.normal(ks[3], (DEPTH, D_MODEL, IN_COLS), jnp.float32) * D_MODEL ** -0.5
    mem_norm_g = 1.0 + 0.02 * jax.random.normal(ks[4], (DEPTH, D_MODEL), jnp.float32)
    w_mem_kv = jax.random.normal(ks[5], (DEPTH, D_MODEL, 2 * MEM_WIDTH), jnp.float32) * D_MODEL ** -0.5
    w_out = jax.random.normal(ks[6], (DEPTH, MIX_WIDTH, D_MODEL), jnp.float32) * MIX_WIDTH ** -0.5
    final_norm_g = 1.0 + 0.02 * jax.random.normal(ks[7], (D_MODEL,), jnp.float32)
    return {"x": x, "mem": mem, "norm_g": norm_g, "w_in": w_in, "mem_norm_g": mem_norm_g,
            "w_mem_kv": w_mem_kv, "w_out": w_out, "final_norm_g": final_norm_g}


def reference(x, mem, norm_g, w_in, mem_norm_g, w_mem_kv, w_out, final_norm_g):
    T = x.shape[1]
    pos = jnp.arange(T)
    widths = [SB_WIDTH] * 4 + [MOBA_WIDTH] * 4 + [MEM_WIDTH] * 2
    split_at = np.cumsum(widths)[:-1].tolist()
    for layer in range(DEPTH):
        h = rms_norm(x, norm_g[layer])
        proj = jnp.einsum('btd,dc->btc', h, w_in[layer])
        sb_q, sb_k, sb_v, sb_g, mb_q, mb_k, mb_v, mb_g, mem_q, mem_g = jnp.split(proj, split_at, axis=-1)

        o_sb = stick_breaking_attention(split_heads(sb_q, SB_HEADS), split_heads(sb_k, SB_HEADS),
                                        split_heads(sb_v, SB_HEADS))
        o_sb = merge_heads(o_sb) * jax.nn.silu(sb_g)

        q_mb = partial_rotary(split_heads(mb_q, MOBA_HEADS), pos)
        k_mb = partial_rotary(split_heads(mb_k, MOBA_HEADS), pos)
        o_mb = moba_attention(q_mb, k_mb, split_heads(mb_v, MOBA_HEADS))
        o_mb = merge_heads(o_mb) * jax.nn.silu(mb_g)

        m = rms_norm(mem, mem_norm_g[layer])
        mkv = jnp.einsum('bmd,dc->bmc', m, w_mem_kv[layer])
        mk, mv = jnp.split(mkv, 2, axis=-1)
        o_mem = memory_attention(split_heads(mem_q, MEM_HEADS), split_heads(mk, MEM_HEADS),
                                 split_heads(mv, MEM_HEADS))
        o_mem = merge_heads(o_mem) * jax.nn.silu(mem_g)

        mixed = jnp.concatenate([o_sb, o_mb, o_mem], axis=-1)
        x = x + jnp.einsum('btc,cd->btd', mixed, w_out[layer])
    return rms_norm(x, final_norm_g)
```

```python
import functools

import numpy as np
import jax
import jax.numpy as jnp
from jax import lax
from jax.experimental import pallas as pl
from jax.experimental.pallas import tpu as pltpu

D_MODEL = 1024
DEPTH = 2
HEAD_DIM = 64
MEM_HEADS = 4
SB_HEADS = 6
MOBA_HEADS = 6
MEM_WIDTH = MEM_HEADS * HEAD_DIM
SB_WIDTH = SB_HEADS * HEAD_DIM
MOBA_WIDTH = MOBA_HEADS * HEAD_DIM
MIX_WIDTH = SB_WIDTH + MOBA_WIDTH + MEM_WIDTH
IN_COLS = 4 * SB_WIDTH + 4 * MOBA_WIDTH + 2 * MEM_WIDTH
MOBA_BLOCK = 256
MOBA_TOPK = 3
ROPE_THETA = 500000.0
ROPE_DIMS = HEAD_DIM // 4
ROPE_HALF = ROPE_DIMS // 2
NORM_EPS = 1e-6
NEG_INF = -1e30
QK_SCALE = HEAD_DIM ** -0.5

LANES = 128
V7X_VMEM_LIMIT = 48 * 1024 * 1024

PAIR = LANES
PROJ_ROWS = 512
PROJ_CHUNK = 512
Q_ROWS = 256
KEY_BLOCK = 256
CUM_BLOCK = 128

F32 = jnp.float32
BF16 = jnp.bfloat16
_NT = (((1,), (1,)), ((), ()))

_GROUPS = (("sb_q", SB_WIDTH), ("sb_k", SB_WIDTH), ("sb_v", SB_WIDTH), ("sb_g", SB_WIDTH),
           ("mb_q", MOBA_WIDTH), ("mb_k", MOBA_WIDTH), ("mb_v", MOBA_WIDTH), ("mb_g", MOBA_WIDTH),
           ("mem_q", MEM_WIDTH), ("mem_g", MEM_WIDTH))
_GATE_OFFSET = {"sb_g": 0, "mb_g": SB_WIDTH, "mem_g": SB_WIDTH + MOBA_WIDTH}


def _unit_table():
    units = []
    for name, width in _GROUPS:
        for off in range(0, width, LANES):
            units.append((name, off))
    return units


_UNITS = _unit_table()


def _rms_norm(x, gain):
    ms = jnp.mean(x * x, axis=-1, keepdims=True)
    return x * lax.rsqrt(ms + NORM_EPS) * gain


def _rotary(y, cos, sin_up, sin_dn):
    return y * cos + pltpu.roll(y, ROPE_HALF, 1) * sin_up + pltpu.roll(y, LANES - ROPE_HALF, 1) * sin_dn


def _project(x, gain_ref, w_ref, cos_ref, sup_ref, sdn_ref, outs):
    h = _rms_norm(x, gain_ref[...]).astype(BF16)
    cos, sup, sdn = cos_ref[...], sup_ref[...], sdn_ref[...]
    units_per_chunk = PROJ_CHUNK // LANES
    for c in range(IN_COLS // PROJ_CHUNK):
        y = jnp.dot(h, w_ref[:, c * PROJ_CHUNK:(c + 1) * PROJ_CHUNK], preferred_element_type=F32)
        for u in range(units_per_chunk):
            name, off = _UNITS[c * units_per_chunk + u]
            piece = y[:, u * LANES:(u + 1) * LANES]
            if name in _GATE_OFFSET:
                g0 = _GATE_OFFSET[name] + off
                outs["gates"][:, g0:g0 + LANES] = piece
                continue
            if name in ("mb_q", "mb_k"):
                piece = _rotary(piece, cos, sup, sdn)
            if name == "mb_k":
                for blk in range(PROJ_ROWS // MOBA_BLOCK):
                    rows = piece[blk * MOBA_BLOCK:(blk + 1) * MOBA_BLOCK]
                    outs["kmean"][0, blk:blk + 1, off:off + LANES] = jnp.mean(rows, axis=0, keepdims=True)
            if name.endswith("_q"):
                piece = piece * QK_SCALE
            outs[name][:, off:off + LANES] = piece.astype(BF16)


_PROJ_OUT_NAMES = ("sb_q", "sb_k", "sb_v", "mb_q", "mb_k", "mb_v", "mem_q", "gates", "kmean")


def _first_proj_kernel(x_ref, gain_ref, w_ref, cos_ref, sup_ref, sdn_ref, *out_refs):
    outs = dict(zip(_PROJ_OUT_NAMES, out_refs))
    _project(x_ref[...], gain_ref, w_ref, cos_ref, sup_ref, sdn_ref, outs)


def _residual(x_ref, osb_ref, omb_ref, omem_ref, wo_ref):
    mixed = jnp.concatenate([osb_ref[...], omb_ref[...], omem_ref[...]], axis=1)
    return x_ref[...] + jnp.dot(mixed, wo_ref[...], preferred_element_type=F32)


def _next_proj_kernel(x_ref, osb_ref, omb_ref, omem_ref, wo_ref, gain_ref, w_ref, cos_ref, sup_ref, sdn_ref,
                      xnew_ref, *out_refs):
    outs = dict(zip(_PROJ_OUT_NAMES, out_refs))
    xnew = _residual(x_ref, osb_ref, omb_ref, omem_ref, wo_ref)
    xnew_ref[...] = xnew
    _project(xnew, gain_ref, w_ref, cos_ref, sup_ref, sdn_ref, outs)


def _final_kernel(x_ref, osb_ref, omb_ref, omem_ref, wo_ref, gain_ref, out_ref):
    out_ref[...] = _rms_norm(_residual(x_ref, osb_ref, omb_ref, omem_ref, wo_ref), gain_ref[...])


def _row_spec(cols):
    return pl.BlockSpec((PROJ_ROWS, cols), lambda r: (r, 0))


def _const_spec(shape):
    return pl.BlockSpec(shape, lambda r: (0,) * len(shape))


def _proj_out(rows):
    shapes = {"sb_q": SB_WIDTH, "sb_k": SB_WIDTH, "sb_v": SB_WIDTH, "mb_q": MOBA_WIDTH, "mb_k": MOBA_WIDTH,
              "mb_v": MOBA_WIDTH, "mem_q": MEM_WIDTH}
    out_shape, out_specs = [], []
    for name in _PROJ_OUT_NAMES:
        if name == "gates":
            out_shape.append(jax.ShapeDtypeStruct((rows, MIX_WIDTH), F32))
            out_specs.append(_row_spec(MIX_WIDTH))
        elif name == "kmean":
            blocks = PROJ_ROWS // MOBA_BLOCK
            out_shape.append(jax.ShapeDtypeStruct((rows // PROJ_ROWS, blocks, MOBA_WIDTH), F32))
            out_specs.append(pl.BlockSpec((1, blocks, MOBA_WIDTH), lambda r: (r, 0, 0)))
        else:
            out_shape.append(jax.ShapeDtypeStruct((rows, shapes[name]), BF16))
            out_specs.append(_row_spec(shapes[name]))
    return out_shape, out_specs


def _rope_specs(seq):
    steps = seq // PROJ_ROWS
    return [pl.BlockSpec((PROJ_ROWS, LANES), lambda r: (r % steps, 0))] * 3


_PARAMS_1D = pltpu.CompilerParams(dimension_semantics=("arbitrary",), vmem_limit_bytes=V7X_VMEM_LIMIT)


def _first_projection(x2d, gain, w, rope, seq):
    rows = x2d.shape[0]
    out_shape, out_specs = _proj_out(rows)
    return pl.pallas_call(
        _first_proj_kernel,
        grid=(rows // PROJ_ROWS,),
        in_specs=[_row_spec(D_MODEL), _const_spec((1, D_MODEL)), _const_spec((D_MODEL, IN_COLS))] + _rope_specs(seq),
        out_specs=out_specs,
        out_shape=out_shape,
        compiler_params=_PARAMS_1D,
    )(x2d, gain, w, *rope)


def _mix_specs():
    return [_row_spec(D_MODEL), _row_spec(SB_WIDTH), _row_spec(MOBA_WIDTH), _row_spec(MEM_WIDTH),
            _const_spec((MIX_WIDTH, D_MODEL))]


def _next_projection(x2d, osb, omb, omem, wo, gain, w, rope, seq):
    rows = x2d.shape[0]
    out_shape, out_specs = _proj_out(rows)
    return pl.pallas_call(
        _next_proj_kernel,
        grid=(rows // PROJ_ROWS,),
        in_specs=_mix_specs() + [_const_spec((1, D_MODEL)), _const_spec((D_MODEL, IN_COLS))] + _rope_specs(seq),
        out_specs=[_row_spec(D_MODEL)] + out_specs,
        out_shape=[jax.ShapeDtypeStruct((rows, D_MODEL), F32)] + out_shape,
        compiler_params=_PARAMS_1D,
    )(x2d, osb, omb, omem, wo, gain, w, *rope)


def _final_projection(x2d, osb, omb, omem, wo, gain):
    rows = x2d.shape[0]
    return pl.pallas_call(
        _final_kernel,
        grid=(rows // PROJ_ROWS,),
        in_specs=_mix_specs() + [_const_spec((1, D_MODEL))],
        out_specs=_row_spec(D_MODEL),
        out_shape=jax.ShapeDtypeStruct((rows, D_MODEL), F32),
        compiler_params=_PARAMS_1D,
    )(x2d, osb, omb, omem, wo, gain)


def _mem_kv_kernel(mem_ref, gain_ref, w_ref, mk_ref, mv_ref):
    m = _rms_norm(mem_ref[0], gain_ref[0]).astype(BF16)
    kv = jnp.dot(m, w_ref[0], preferred_element_type=F32)
    mk_ref[0, 0] = kv[:, :MEM_WIDTH].astype(BF16)
    mv_ref[0, 0] = kv[:, MEM_WIDTH:].astype(BF16)


def _mem_kv(mem, gains, w):
    batch, mem_len, _ = mem.shape
    out = jax.ShapeDtypeStruct((DEPTH, batch, mem_len, MEM_WIDTH), BF16)
    out_spec = pl.BlockSpec((1, 1, mem_len, MEM_WIDTH), lambda l, b: (l, b, 0, 0))
    return pl.pallas_call(
        _mem_kv_kernel,
        grid=(DEPTH, batch),
        in_specs=[pl.BlockSpec((1, mem_len, D_MODEL), lambda l, b: (b, 0, 0)),
                  pl.BlockSpec((1, 1, D_MODEL), lambda l, b: (l, 0, 0)),
                  pl.BlockSpec((1, D_MODEL, 2 * MEM_WIDTH), lambda l, b: (l, 0, 0))],
        out_specs=[out_spec, out_spec],
        out_shape=[out, out],
        compiler_params=pltpu.CompilerParams(dimension_semantics=("arbitrary", "arbitrary"),
                                             vmem_limit_bytes=V7X_VMEM_LIMIT),
    )(mem, gains, w)


def _stack_heads(q_pair):
    lane = lax.broadcasted_iota(jnp.int32, q_pair.shape, 1)
    zero = jnp.zeros_like(q_pair)
    return jnp.concatenate([jnp.where(lane < HEAD_DIM, q_pair, zero),
                            jnp.where(lane >= HEAD_DIM, q_pair, zero)], axis=0)


def _unstack_heads(o):
    rows = o.shape[0] // 2
    lane = lax.broadcasted_iota(jnp.int32, (rows, PAIR), 1)
    return jnp.where(lane < HEAD_DIM, o[:rows], o[rows:])


def _silu(g):
    return g * jax.nn.sigmoid(g)


def _local_positions(keys):
    row = lax.broadcasted_iota(jnp.int32, (2 * Q_ROWS, keys), 0) % Q_ROWS
    col = lax.broadcasted_iota(jnp.int32, (2 * Q_ROWS, keys), 1)
    return row, col


def _sb_kernel(q_ref, k_ref, v_ref, g_ref, cum_ref, o_ref, acc_ref, tail_ref):
    i = pl.program_id(2)
    qs = _stack_heads(q_ref[0])
    cum = cum_ref[...]
    row, col = _local_positions(CUM_BLOCK)
    acc_ref[...] = jnp.zeros_like(acc_ref)
    tail_ref[...] = jnp.zeros_like(tail_ref)

    def key_block(kb, diagonal):
        start = pl.multiple_of(kb * KEY_BLOCK, KEY_BLOCK)
        z = lax.dot_general(qs, k_ref[0, pl.ds(start, KEY_BLOCK), :], _NT, preferred_element_type=F32)
        tail = tail_ref[...]
        weights = [None] * (KEY_BLOCK // CUM_BLOCK)
        for part in reversed(range(KEY_BLOCK // CUM_BLOCK)):
            cols = slice(part * CUM_BLOCK, (part + 1) * CUM_BLOCK)
            zp = z[:, cols]
            log_keep = jnp.minimum(-zp, 0.0) - jnp.log1p(jnp.exp(-jnp.abs(zp)))
            if diagonal:
                past = col + part * CUM_BLOCK < row
                log_keep = jnp.where(past, log_keep, 0.0)
            hi = log_keep.astype(BF16)
            lo = (log_keep - hi.astype(F32)).astype(BF16)
            sums = jnp.dot(jnp.concatenate([hi, lo], axis=1), cum, preferred_element_type=F32)
            w = jnp.exp(zp + sums[:, :CUM_BLOCK] + tail)
            if diagonal:
                w = jnp.where(past, w, 0.0)
            weights[part] = w.astype(BF16)
            tail = tail + sums[:, CUM_BLOCK:]
        tail_ref[...] = tail
        acc_ref[...] += jnp.dot(jnp.concatenate(weights, axis=1), v_ref[0, pl.ds(start, KEY_BLOCK), :],
                                preferred_element_type=F32)

    key_block(i, True)

    def body(n, carry):
        key_block(i - 1 - n, False)
        return carry

    lax.fori_loop(0, i, body, 0)
    o_ref[0] = (_unstack_heads(acc_ref[...]) * _silu(g_ref[0])).astype(BF16)


def _softmax_block(s, v_blk, m_ref, acc_ref, first):
    v_ones = jnp.concatenate([v_blk, jnp.ones_like(v_blk)], axis=1)
    m_blk = jnp.max(s, axis=-1, keepdims=True)
    if first:
        m_new = m_blk
    else:
        m_old = m_ref[...]
        m_new = jnp.maximum(m_old, m_blk)
    p = jnp.exp(s - m_new).astype(BF16)
    pv = jnp.dot(p, v_ones, preferred_element_type=F32)
    if first:
        acc_ref[...] = pv
    else:
        acc_ref[...] = acc_ref[...] * jnp.exp(m_old - m_new) + pv
    m_ref[...] = m_new


def _softmax_result(acc_ref):
    acc = acc_ref[...]
    return _unstack_heads(acc[:, :PAIR] / acc[:, PAIR:])


def _moba_kernel(q_ref, k_ref, v_ref, kmean_ref, g_ref, o_ref, acc_ref, m_ref):
    i = pl.program_id(2)
    qs = _stack_heads(q_ref[0])
    row, col = _local_positions(KEY_BLOCK)

    km = kmean_ref[0]
    km_hi = km.astype(BF16).astype(F32)
    pad = jnp.zeros((LANES - km.shape[0], LANES), F32)
    gate = jnp.zeros((2 * Q_ROWS, LANES), F32)
    for piece in (km_hi, km - km_hi):
        gate = gate + lax.dot_general(qs, jnp.concatenate([piece, pad], axis=0).astype(BF16), _NT,
                                      preferred_element_type=F32)
    lane = lax.broadcasted_iota(jnp.int32, gate.shape, 1)
    gate = jnp.where(lane < i, gate, -jnp.inf)
    ahead = jnp.zeros_like(gate)
    for j in range(kmean_ref.shape[1] - 1):
        gj = gate[:, j:j + 1]
        ahead = ahead + jnp.where((gj > gate) | ((gj == gate) & (lane > j)), 1.0, 0.0)
    selected = jnp.where((ahead < MOBA_TOPK) & (lane < i), 1.0, 0.0)

    own = pl.multiple_of(i * KEY_BLOCK, KEY_BLOCK)
    s = lax.dot_general(qs, k_ref[0, pl.ds(own, KEY_BLOCK), :], _NT, preferred_element_type=F32)
    s = jnp.where(col <= row, s, NEG_INF)
    _softmax_block(s, v_ref[0, pl.ds(own, KEY_BLOCK), :], m_ref, acc_ref, first=True)

    def body(j, carry):
        start = pl.multiple_of(j * KEY_BLOCK, KEY_BLOCK)
        chosen = jnp.sum(jnp.where(lane == j, selected, 0.0), axis=-1, keepdims=True) > 0.5
        s = lax.dot_general(qs, k_ref[0, pl.ds(start, KEY_BLOCK), :], _NT, preferred_element_type=F32)
        s = jnp.where(chosen, s, NEG_INF)
        _softmax_block(s, v_ref[0, pl.ds(start, KEY_BLOCK), :], m_ref, acc_ref, first=False)
        return carry

    lax.fori_loop(0, i, body, 0)
    o_ref[0] = (_softmax_result(acc_ref) * _silu(g_ref[0])).astype(BF16)


def _mem_kernel(q_ref, mk_ref, mv_ref, g_ref, o_ref, acc_ref, m_ref):
    qs = _stack_heads(q_ref[0])
    s = lax.dot_general(qs, mk_ref[0, 0], _NT, preferred_element_type=F32)
    _softmax_block(s, mv_ref[0, 0], m_ref, acc_ref, first=True)
    o_ref[0] = (_softmax_result(acc_ref) * _silu(g_ref[0])).astype(BF16)


_PARAMS_3D = pltpu.CompilerParams(dimension_semantics=("arbitrary", "arbitrary", "arbitrary"),
                                  vmem_limit_bytes=V7X_VMEM_LIMIT)


def _q_spec(col_block_offset=0):
    return pl.BlockSpec((1, Q_ROWS, PAIR), lambda b, p, i: (b, i, p + col_block_offset))


def _kv_spec(seq):
    return pl.BlockSpec((1, seq, PAIR), lambda b, p, i: (b, 0, p))


def _cum_matrix():
    j = np.arange(CUM_BLOCK)
    tri = (j[:, None] >= j[None, :]).astype(np.float32)
    half = np.concatenate([tri, np.ones((CUM_BLOCK, CUM_BLOCK), np.float32)], axis=1)
    return jnp.asarray(np.concatenate([half, half], axis=0), dtype=BF16)


def _sb_attention(q, k, v, gates):
    batch, seq, width = q.shape
    grid = (batch, width // PAIR, seq // Q_ROWS)
    return pl.pallas_call(
        _sb_kernel,
        grid=grid,
        in_specs=[_q_spec(), _kv_spec(seq), _kv_spec(seq), _q_spec(_GATE_OFFSET["sb_g"] // PAIR),
                  pl.BlockSpec((2 * CUM_BLOCK, 2 * CUM_BLOCK), lambda b, p, i: (0, 0))],
        out_specs=_q_spec(),
        out_shape=jax.ShapeDtypeStruct(q.shape, BF16),
        scratch_shapes=[pltpu.VMEM((2 * Q_ROWS, PAIR), F32), pltpu.VMEM((2 * Q_ROWS, CUM_BLOCK), F32)],
        compiler_params=_PARAMS_3D,
    )(q, k, v, gates, _cum_matrix())


def _softmax_scratch():
    return [pltpu.VMEM((2 * Q_ROWS, 2 * PAIR), F32), pltpu.VMEM((2 * Q_ROWS, 1), F32)]


def _moba_attention(q, k, v, kmean, gates):
    batch, seq, width = q.shape
    grid = (batch, width // PAIR, seq // Q_ROWS)
    return pl.pallas_call(
        _moba_kernel,
        grid=grid,
        in_specs=[_q_spec(), _kv_spec(seq), _kv_spec(seq),
                  pl.BlockSpec((1, seq // MOBA_BLOCK, PAIR), lambda b, p, i: (b, 0, p)),
                  _q_spec(_GATE_OFFSET["mb_g"] // PAIR)],
        out_specs=_q_spec(),
        out_shape=jax.ShapeDtypeStruct(q.shape, BF16),
        scratch_shapes=_softmax_scratch(),
        compiler_params=_PARAMS_3D,
    )(q, k, v, kmean, gates)


def _mem_attention(q, mk, mv, gates, layer):
    batch, seq, width = q.shape
    mem_len = mk.shape[2]
    grid = (batch, width // PAIR, seq // Q_ROWS)
    mem_spec = pl.BlockSpec((1, 1, mem_len, PAIR), lambda b, p, i: (layer, b, 0, p))
    return pl.pallas_call(
        _mem_kernel,
        grid=grid,
        in_specs=[_q_spec(), mem_spec, mem_spec, _q_spec(_GATE_OFFSET["mem_g"] // PAIR)],
        out_specs=_q_spec(),
        out_shape=jax.ShapeDtypeStruct(q.shape, BF16),
        scratch_shapes=_softmax_scratch(),
        compiler_params=_PARAMS_3D,
    )(q, mk, mv, gates)


def _rope_tables(seq):
    inv_freq = jnp.float32(ROPE_THETA) ** (-jnp.arange(ROPE_HALF, dtype=F32) * 2.0 / ROPE_DIMS)
    ang = jnp.arange(seq).astype(F32)[:, None] * inv_freq[None, :]
    cos, sin = jnp.cos(ang), jnp.sin(ang)
    d = np.arange(LANES) % HEAD_DIM
    f = d % ROPE_HALF
    cos_t = jnp.where(d < ROPE_DIMS, cos[:, f], 1.0)
    sin_up = jnp.where((d >= ROPE_HALF) & (d < ROPE_DIMS), sin[:, f], 0.0)
    sin_dn = jnp.where(d < ROPE_HALF, -sin[:, f], 0.0)
    return cos_t, sin_up, sin_dn


def kernel(x, mem, norm_g, w_in, mem_norm_g, w_mem_kv, w_out, final_norm_g):
    batch, seq, d_model = x.shape
    assert d_model == D_MODEL and seq % PROJ_ROWS == 0 and seq % Q_ROWS == 0
    rows = batch * seq
    w_in_b, w_out_b, w_kv_b = w_in.astype(BF16), w_out.astype(BF16), w_mem_kv.astype(BF16)
    rope = _rope_tables(seq)
    mk, mv = _mem_kv(mem, mem_norm_g.reshape(DEPTH, 1, D_MODEL), w_kv_b)

    x2d = x.reshape(rows, D_MODEL)
    proj = _first_projection(x2d, norm_g[0:1], w_in_b[0], rope, seq)
    for layer in range(DEPTH):
        p = dict(zip(_PROJ_OUT_NAMES, proj))
        seq3 = lambda a: a.reshape(batch, seq, a.shape[-1])
        gates = seq3(p["gates"])
        o_sb = _sb_attention(seq3(p["sb_q"]), seq3(p["sb_k"]), seq3(p["sb_v"]), gates)
        kmean = p["kmean"].reshape(batch, seq // MOBA_BLOCK, MOBA_WIDTH)
        o_mb = _moba_attention(seq3(p["mb_q"]), seq3(p["mb_k"]), seq3(p["mb_v"]), kmean, gates)
        o_mem = _mem_attention(seq3(p["mem_q"]), mk, mv, gates, layer)
        mixed = [o.reshape(rows, o.shape[-1]) for o in (o_sb, o_mb, o_mem)]
        if layer + 1 < DEPTH:
            x2d, *proj = _next_projection(x2d, *mixed, w_out_b[layer], norm_g[layer + 1:layer + 2],
                                          w_in_b[layer + 1], rope, seq)
        else:
            out = _final_projection(x2d, *mixed, w_out_b[layer], final_norm_g.reshape(1, D_MODEL))
    return out.reshape(batch, seq, D_MODEL)
```

```python
import functools

import numpy as np
import jax
import jax.numpy as jnp
from jax import lax
from jax.experimental import pallas as pl
from jax.experimental.pallas import tpu as pltpu

D_MODEL = 1024
DEPTH = 2
HEAD_DIM = 64
MEM_HEADS = 4
SB_HEADS = 6
MOBA_HEADS = 6
MEM_WIDTH = MEM_HEADS * HEAD_DIM
SB_WIDTH = SB_HEADS * HEAD_DIM
MOBA_WIDTH = MOBA_HEADS * HEAD_DIM
MIX_WIDTH = SB_WIDTH + MOBA_WIDTH + MEM_WIDTH
IN_COLS = 4 * SB_WIDTH + 4 * MOBA_WIDTH + 2 * MEM_WIDTH
MOBA_BLOCK = 256
MOBA_TOPK = 3
ROPE_THETA = 500000.0
ROPE_DIMS = HEAD_DIM // 4
ROPE_HALF = ROPE_DIMS // 2
NORM_EPS = 1e-6
NEG_INF = -1e30
QK_SCALE = HEAD_DIM ** -0.5

LANES = 128
V7X_VMEM_LIMIT = 48 * 1024 * 1024

PAIR = LANES
PROJ_ROWS = 512
PROJ_CHUNK = 512
Q_ROWS = 256
KEY_BLOCK = 256
CUM_BLOCK = 128
MEM_Q_ROWS = 1024

F32 = jnp.float32
BF16 = jnp.bfloat16
_NT = (((1,), (1,)), ((), ()))

_GROUPS = (("sb_q", SB_WIDTH), ("sb_k", SB_WIDTH), ("sb_v", SB_WIDTH), ("sb_g", SB_WIDTH),
           ("mb_q", MOBA_WIDTH), ("mb_k", MOBA_WIDTH), ("mb_v", MOBA_WIDTH), ("mb_g", MOBA_WIDTH),
           ("mem_q", MEM_WIDTH), ("mem_g", MEM_WIDTH))
_GATE_OFFSET = {"sb_g": 0, "mb_g": SB_WIDTH, "mem_g": SB_WIDTH + MOBA_WIDTH}


def _unit_table():
    units = []
    for name, width in _GROUPS:
        for off in range(0, width, LANES):
            units.append((name, off))
    return units


_UNITS = _unit_table()


def _rms_norm(x, gain):
    ms = jnp.mean(x * x, axis=-1, keepdims=True)
    return x * lax.rsqrt(ms + NORM_EPS) * gain


def _rotary(y, cos, sin_up, sin_dn):
    return y * cos + pltpu.roll(y, ROPE_HALF, 1) * sin_up + pltpu.roll(y, LANES - ROPE_HALF, 1) * sin_dn


def _project(x, gain_ref, w_ref, cos_ref, sup_ref, sdn_ref, outs):
    h = _rms_norm(x, gain_ref[...]).astype(BF16)
    cos, sup, sdn = cos_ref[...], sup_ref[...], sdn_ref[...]
    units_per_chunk = PROJ_CHUNK // LANES
    for c in range(IN_COLS // PROJ_CHUNK):
        y = jnp.dot(h, w_ref[:, c * PROJ_CHUNK:(c + 1) * PROJ_CHUNK], preferred_element_type=F32)
        for u in range(units_per_chunk):
            name, off = _UNITS[c * units_per_chunk + u]
            piece = y[:, u * LANES:(u + 1) * LANES]
            if name in _GATE_OFFSET:
                g0 = _GATE_OFFSET[name] + off
                outs["gates"][:, g0:g0 + LANES] = piece
                continue
            if name in ("mb_q", "mb_k"):
                piece = _rotary(piece, cos, sup, sdn)
            if name == "mb_k":
                for blk in range(PROJ_ROWS // MOBA_BLOCK):
                    rows = piece[blk * MOBA_BLOCK:(blk + 1) * MOBA_BLOCK]
                    outs["kmean"][0, blk:blk + 1, off:off + LANES] = jnp.mean(rows, axis=0, keepdims=True)
            if name.endswith("_q"):
                piece = piece * QK_SCALE
            outs[name][:, off:off + LANES] = piece.astype(BF16)


_PROJ_OUT_NAMES = ("sb_q", "sb_k", "sb_v", "mb_q", "mb_k", "mb_v", "mem_q", "gates", "kmean")


def _first_proj_kernel(x_ref, gain_ref, w_ref, cos_ref, sup_ref, sdn_ref, *out_refs):
    outs = dict(zip(_PROJ_OUT_NAMES, out_refs))
    _project(x_ref[...], gain_ref, w_ref, cos_ref, sup_ref, sdn_ref, outs)


def _residual(x_ref, osb_ref, omb_ref, omem_ref, wo_ref):
    mixed = jnp.concatenate([osb_ref[...], omb_ref[...], omem_ref[...]], axis=1)
    return x_ref[...] + jnp.dot(mixed, wo_ref[...], preferred_element_type=F32)


def _next_proj_kernel(x_ref, osb_ref, omb_ref, omem_ref, wo_ref, gain_ref, w_ref, cos_ref, sup_ref, sdn_ref,
                      xnew_ref, *out_refs):
    outs = dict(zip(_PROJ_OUT_NAMES, out_refs))
    xnew = _residual(x_ref, osb_ref, omb_ref, omem_ref, wo_ref)
    xnew_ref[...] = xnew
    _project(xnew, gain_ref, w_ref, cos_ref, sup_ref, sdn_ref, outs)


def _final_kernel(x_ref, osb_ref, omb_ref, omem_ref, wo_ref, gain_ref, out_ref):
    out_ref[...] = _rms_norm(_residual(x_ref, osb_ref, omb_ref, omem_ref, wo_ref), gain_ref[...])


def _row_spec(cols):
    return pl.BlockSpec((PROJ_ROWS, cols), lambda r: (r, 0))


def _const_spec(shape):
    return pl.BlockSpec(shape, lambda r: (0,) * len(shape))


def _proj_out(rows):
    shapes = {"sb_q": SB_WIDTH, "sb_k": SB_WIDTH, "sb_v": SB_WIDTH, "mb_q": MOBA_WIDTH, "mb_k": MOBA_WIDTH,
              "mb_v": MOBA_WIDTH, "mem_q": MEM_WIDTH}
    out_shape, out_specs = [], []
    for name in _PROJ_OUT_NAMES:
        if name == "gates":
            out_shape.append(jax.ShapeDtypeStruct((rows, MIX_WIDTH), F32))
            out_specs.append(_row_spec(MIX_WIDTH))
        elif name == "kmean":
            blocks = PROJ_ROWS // MOBA_BLOCK
            out_shape.append(jax.ShapeDtypeStruct((rows // PROJ_ROWS, blocks, MOBA_WIDTH), F32))
            out_specs.append(pl.BlockSpec((1, blocks, MOBA_WIDTH), lambda r: (r, 0, 0)))
        else:
            out_shape.append(jax.ShapeDtypeStruct((rows, shapes[name]), BF16))
            out_specs.append(_row_spec(shapes[name]))
    return out_shape, out_specs


def _rope_specs(seq):
    steps = seq // PROJ_ROWS
    return [pl.BlockSpec((PROJ_ROWS, LANES), lambda r: (r % steps, 0))] * 3


_PARAMS_1D = pltpu.CompilerParams(dimension_semantics=("arbitrary",), vmem_limit_bytes=V7X_VMEM_LIMIT)


def _first_projection(x2d, gain, w, rope, seq):
    rows = x2d.shape[0]
    out_shape, out_specs = _proj_out(rows)
    return pl.pallas_call(
        _first_proj_kernel,
        grid=(rows // PROJ_ROWS,),
        in_specs=[_row_spec(D_MODEL), _const_spec((1, D_MODEL)), _const_spec((D_MODEL, IN_COLS))] + _rope_specs(seq),
        out_specs=out_specs,
        out_shape=out_shape,
        compiler_params=_PARAMS_1D,
        name="first_projection",
    )(x2d, gain, w, *rope)


def _mix_specs():
    return [_row_spec(D_MODEL), _row_spec(SB_WIDTH), _row_spec(MOBA_WIDTH), _row_spec(MEM_WIDTH),
            _const_spec((MIX_WIDTH, D_MODEL))]


def _next_projection(x2d, osb, omb, omem, wo, gain, w, rope, seq):
    rows = x2d.shape[0]
    out_shape, out_specs = _proj_out(rows)
    return pl.pallas_call(
        _next_proj_kernel,
        grid=(rows // PROJ_ROWS,),
        in_specs=_mix_specs() + [_const_spec((1, D_MODEL)), _const_spec((D_MODEL, IN_COLS))] + _rope_specs(seq),
        out_specs=[_row_spec(D_MODEL)] + out_specs,
        out_shape=[jax.ShapeDtypeStruct((rows, D_MODEL), F32)] + out_shape,
        compiler_params=_PARAMS_1D,
        name="next_projection",
    )(x2d, osb, omb, omem, wo, gain, w, *rope)


def _final_projection(x2d, osb, omb, omem, wo, gain):
    rows = x2d.shape[0]
    return pl.pallas_call(
        _final_kernel,
        grid=(rows // PROJ_ROWS,),
        in_specs=_mix_specs() + [_const_spec((1, D_MODEL))],
        out_specs=_row_spec(D_MODEL),
        out_shape=jax.ShapeDtypeStruct((rows, D_MODEL), F32),
        compiler_params=_PARAMS_1D,
        name="final_projection",
    )(x2d, osb, omb, omem, wo, gain)


def _mem_kv_kernel(mem_ref, gain_ref, w_ref, mk_ref, mv_ref):
    m = _rms_norm(mem_ref[0], gain_ref[0]).astype(BF16)
    kv = jnp.dot(m, w_ref[0], preferred_element_type=F32)
    mk_ref[0, 0] = kv[:, :MEM_WIDTH].astype(BF16)
    mv_ref[0, 0] = kv[:, MEM_WIDTH:].astype(BF16)


def _mem_kv(mem, gains, w):
    batch, mem_len, _ = mem.shape
    out = jax.ShapeDtypeStruct((DEPTH, batch, mem_len, MEM_WIDTH), BF16)
    out_spec = pl.BlockSpec((1, 1, mem_len, MEM_WIDTH), lambda l, b: (l, b, 0, 0))
    return pl.pallas_call(
        _mem_kv_kernel,
        grid=(DEPTH, batch),
        in_specs=[pl.BlockSpec((1, mem_len, D_MODEL), lambda l, b: (b, 0, 0)),
                  pl.BlockSpec((1, 1, D_MODEL), lambda l, b: (l, 0, 0)),
                  pl.BlockSpec((1, D_MODEL, 2 * MEM_WIDTH), lambda l, b: (l, 0, 0))],
        out_specs=[out_spec, out_spec],
        out_shape=[out, out],
        compiler_params=pltpu.CompilerParams(dimension_semantics=("arbitrary", "arbitrary"),
                                             vmem_limit_bytes=V7X_VMEM_LIMIT),
        name="memory_kv",
    )(mem, gains, w)


def _stack_heads(q_pair):
    lane = lax.broadcasted_iota(jnp.int32, q_pair.shape, 1)
    zero = jnp.zeros_like(q_pair)
    return jnp.concatenate([jnp.where(lane < HEAD_DIM, q_pair, zero),
                            jnp.where(lane >= HEAD_DIM, q_pair, zero)], axis=0)


def _unstack_heads(o):
    rows = o.shape[0] // 2
    lane = lax.broadcasted_iota(jnp.int32, (rows, PAIR), 1)
    return jnp.where(lane < HEAD_DIM, o[:rows], o[rows:])


def _silu(g):
    return g * jax.nn.sigmoid(g)


def _local_positions(keys):
    row = lax.broadcasted_iota(jnp.int32, (2 * Q_ROWS, keys), 0) % Q_ROWS
    col = lax.broadcasted_iota(jnp.int32, (2 * Q_ROWS, keys), 1)
    return row, col


def _for_query_block(i, n_blocks, variant):
    for n in range(n_blocks):
        pl.when(i == n)(functools.partial(variant, n))


def _log_keep(z):
    return jnp.minimum(-z, 0.0) - jnp.log(1.0 + jnp.exp(-jnp.abs(z)))


def _sb_kernel(q_ref, k_ref, v_ref, g_ref, cum_ref, o_ref):
    qs = _stack_heads(q_ref[0])
    row, col = _local_positions(CUM_BLOCK)
    stacked = 2 * Q_ROWS

    def variant(n):
        keys = (n + 1) * KEY_BLOCK
        chunks = keys // CUM_BLOCK
        z = lax.dot_general(qs, k_ref[0, :keys, :], _NT, preferred_element_type=F32)
        pieces, past = [], {}
        for c in range(chunks):
            log_keep = _log_keep(z[:, c * CUM_BLOCK:(c + 1) * CUM_BLOCK])
            if c * CUM_BLOCK >= n * KEY_BLOCK:
                past[c] = col + (c * CUM_BLOCK - n * KEY_BLOCK) < row
                log_keep = jnp.where(past[c], log_keep, 0.0)
            hi = log_keep.astype(BF16)
            lo = (log_keep - hi.astype(F32)).astype(BF16)
            pieces.append(jnp.concatenate([hi, lo], axis=1))
        sums = jnp.dot(jnp.concatenate(pieces, axis=0), cum_ref[...], preferred_element_type=F32)
        tail = None
        weights = [None] * chunks
        for c in reversed(range(chunks)):
            sc = sums[c * stacked:(c + 1) * stacked]
            e = z[:, c * CUM_BLOCK:(c + 1) * CUM_BLOCK] + sc[:, :CUM_BLOCK]
            if tail is not None:
                e = e + tail
            w = jnp.exp(e)
            if c in past:
                w = jnp.where(past[c], w, 0.0)
            weights[c] = w.astype(BF16)
            tail = sc[:, CUM_BLOCK:] if tail is None else tail + sc[:, CUM_BLOCK:]
        acc = jnp.dot(jnp.concatenate(weights, axis=1), v_ref[0, :keys, :], preferred_element_type=F32)
        o_ref[0] = (_unstack_heads(acc) * _silu(g_ref[0])).astype(BF16)

    _for_query_block(pl.program_id(2), k_ref.shape[1] // KEY_BLOCK, variant)


def _softmax_pv(blocks, v_ones):
    m = functools.reduce(jnp.maximum, blocks)
    m = jnp.max(jnp.maximum(m[:, :LANES], m[:, LANES:]), axis=-1, keepdims=True)
    p = jnp.concatenate([jnp.exp(b - m).astype(BF16) for b in blocks], axis=1)
    acc = jnp.dot(p, v_ones, preferred_element_type=F32)
    return _unstack_heads(acc[:, :PAIR] / acc[:, PAIR:])


def _unselected(qs, kmean, n):
    blocks = kmean.shape[0]
    km_hi = kmean.astype(BF16).astype(F32)
    pieces = jnp.concatenate([km_hi, kmean - km_hi], axis=0).astype(BF16)
    g2 = lax.dot_general(pieces, qs, _NT, preferred_element_type=F32)
    gate = g2[:blocks] + g2[blocks:]
    blk = lax.broadcasted_iota(jnp.int32, gate.shape, 0)
    gate = jnp.where(blk < n, gate, -jnp.inf)
    ahead = jnp.zeros_like(gate)
    for j in range(n):
        gj = gate[j:j + 1, :]
        ahead = ahead + jnp.where((gj > gate) | ((gj == gate) & (blk > j)), 1.0, 0.0)
    unsel = jnp.where((ahead >= MOBA_TOPK) & (blk < n), 1.0, 0.0)
    unsel = jnp.concatenate([unsel, jnp.zeros((LANES - blocks, unsel.shape[1]), F32)], axis=0)
    return unsel.T.astype(BF16)


def _moba_kernel(q_ref, k_ref, v_ref, kmean_ref, ind_ref, g_ref, o_ref, kaug_ref, vaug_ref):
    i = pl.program_id(2)

    @pl.when(i == 0)
    def _():
        kaug_ref[:, :PAIR] = k_ref[0]
        kaug_ref[:, PAIR:] = ind_ref[...]
        vaug_ref[:, :PAIR] = v_ref[0]
        vaug_ref[:, PAIR:] = jnp.ones(v_ref.shape[1:], BF16)

    qs = _stack_heads(q_ref[0])
    row, col = _local_positions(KEY_BLOCK)

    def variant(n):
        keys = (n + 1) * KEY_BLOCK
        if n <= MOBA_TOPK:
            s = lax.dot_general(qs, k_ref[0, :keys, :], _NT, preferred_element_type=F32)
        else:
            lhs = jnp.concatenate([qs, _unselected(qs, kmean_ref[0], n)], axis=1)
            s = lax.dot_general(lhs, kaug_ref[:keys, :], _NT, preferred_element_type=F32)
        own = jnp.where(col <= row, s[:, n * KEY_BLOCK:], NEG_INF)
        blocks = [s[:, j * KEY_BLOCK:(j + 1) * KEY_BLOCK] for j in range(n)] + [own]
        o_ref[0] = (_softmax_pv(blocks, vaug_ref[:keys, :]) * _silu(g_ref[0])).astype(BF16)

    _for_query_block(i, k_ref.shape[1] // KEY_BLOCK, variant)


def _mem_kernel(q_ref, mk_ref, mv_ref, g_ref, o_ref):
    mv = mv_ref[0, 0]
    v_ones = jnp.concatenate([mv, jnp.ones_like(mv)], axis=1)
    for r in range(MEM_Q_ROWS // Q_ROWS):
        rows = slice(r * Q_ROWS, (r + 1) * Q_ROWS)
        s = lax.dot_general(_stack_heads(q_ref[0, rows, :]), mk_ref[0, 0], _NT, preferred_element_type=F32)
        o_ref[0, rows, :] = (_softmax_pv([s], v_ones) * _silu(g_ref[0, rows, :])).astype(BF16)


_PARAMS_3D = pltpu.CompilerParams(dimension_semantics=("arbitrary", "arbitrary", "arbitrary"),
                                  vmem_limit_bytes=V7X_VMEM_LIMIT)


def _q_spec(col_block_offset=0, rows=Q_ROWS):
    return pl.BlockSpec((1, rows, PAIR), lambda b, p, i: (b, i, p + col_block_offset))


def _kv_spec(seq):
    return pl.BlockSpec((1, seq, PAIR), lambda b, p, i: (b, 0, p))


def _cum_matrix():
    j = np.arange(CUM_BLOCK)
    tri = (j[:, None] >= j[None, :]).astype(np.float32)
    half = np.concatenate([tri, np.ones((CUM_BLOCK, CUM_BLOCK), np.float32)], axis=1)
    return jnp.asarray(np.concatenate([half, half], axis=0), dtype=BF16)


def _block_indicator(seq):
    s = np.arange(seq)[:, None] // MOBA_BLOCK
    return jnp.asarray(np.where(s == np.arange(LANES)[None, :], NEG_INF, 0.0), dtype=BF16)


def _sb_attention(q, k, v, gates):
    batch, seq, width = q.shape
    grid = (batch, width // PAIR, seq // Q_ROWS)
    return pl.pallas_call(
        _sb_kernel,
        grid=grid,
        in_specs=[_q_spec(), _kv_spec(seq), _kv_spec(seq), _q_spec(_GATE_OFFSET["sb_g"] // PAIR),
                  pl.BlockSpec((2 * CUM_BLOCK, 2 * CUM_BLOCK), lambda b, p, i: (0, 0))],
        out_specs=_q_spec(),
        out_shape=jax.ShapeDtypeStruct(q.shape, BF16),
        compiler_params=_PARAMS_3D,
        name="stick_breaking_attention",
    )(q, k, v, gates, _cum_matrix())


def _moba_attention(q, k, v, kmean, gates):
    batch, seq, width = q.shape
    grid = (batch, width // PAIR, seq // Q_ROWS)
    return pl.pallas_call(
        _moba_kernel,
        grid=grid,
        in_specs=[_q_spec(), _kv_spec(seq), _kv_spec(seq),
                  pl.BlockSpec((1, seq // MOBA_BLOCK, PAIR), lambda b, p, i: (b, 0, p)),
                  pl.BlockSpec((seq, LANES), lambda b, p, i: (0, 0)),
                  _q_spec(_GATE_OFFSET["mb_g"] // PAIR)],
        out_specs=_q_spec(),
        out_shape=jax.ShapeDtypeStruct(q.shape, BF16),
        scratch_shapes=[pltpu.VMEM((seq, 2 * PAIR), BF16), pltpu.VMEM((seq, 2 * PAIR), BF16)],
        compiler_params=_PARAMS_3D,
        name="moba_attention",
    )(q, k, v, kmean, _block_indicator(seq), gates)


def _mem_attention(q, mk, mv, gates, layer):
    batch, seq, width = q.shape
    mem_len = mk.shape[2]
    grid = (batch, width // PAIR, seq // MEM_Q_ROWS)
    mem_spec = pl.BlockSpec((1, 1, mem_len, PAIR), lambda b, p, i: (layer, b, 0, p))
    return pl.pallas_call(
        _mem_kernel,
        grid=grid,
        in_specs=[_q_spec(rows=MEM_Q_ROWS), mem_spec, mem_spec,
                  _q_spec(_GATE_OFFSET["mem_g"] // PAIR, rows=MEM_Q_ROWS)],
        out_specs=_q_spec(rows=MEM_Q_ROWS),
        out_shape=jax.ShapeDtypeStruct(q.shape, BF16),
        compiler_params=_PARAMS_3D,
        name="memory_attention",
    )(q, mk, mv, gates)


def _rope_tables(seq):
    inv_freq = jnp.float32(ROPE_THETA) ** (-jnp.arange(ROPE_HALF, dtype=F32) * 2.0 / ROPE_DIMS)
    ang = jnp.arange(seq).astype(F32)[:, None] * inv_freq[None, :]
    cos, sin = jnp.cos(ang), jnp.sin(ang)
    d = np.arange(LANES) % HEAD_DIM
    f = d % ROPE_HALF
    cos_t = jnp.where(d < ROPE_DIMS, cos[:, f], 1.0)
    sin_up = jnp.where((d >= ROPE_HALF) & (d < ROPE_DIMS), sin[:, f], 0.0)
    sin_dn = jnp.where(d < ROPE_HALF, -sin[:, f], 0.0)
    return cos_t, sin_up, sin_dn


def kernel(x, mem, norm_g, w_in, mem_norm_g, w_mem_kv, w_out, final_norm_g):
    batch, seq, d_model = x.shape
    assert d_model == D_MODEL and seq % PROJ_ROWS == 0 and seq % MEM_Q_ROWS == 0
    assert seq // MOBA_BLOCK <= LANES
    rows = batch * seq
    w_in_b, w_out_b, w_kv_b = w_in.astype(BF16), w_out.astype(BF16), w_mem_kv.astype(BF16)
    rope = _rope_tables(seq)
    mk, mv = _mem_kv(mem, mem_norm_g.reshape(DEPTH, 1, D_MODEL), w_kv_b)

    x2d = x.reshape(rows, D_MODEL)
    proj = _first_projection(x2d, norm_g[0:1], w_in_b[0], rope, seq)
    for layer in range(DEPTH):
        p = dict(zip(_PROJ_OUT_NAMES, proj))
        seq3 = lambda a: a.reshape(batch, seq, a.shape[-1])
        gates = seq3(p["gates"])
        o_sb = _sb_attention(seq3(p["sb_q"]), seq3(p["sb_k"]), seq3(p["sb_v"]), gates)
        kmean = p["kmean"].reshape(batch, seq // MOBA_BLOCK, MOBA_WIDTH)
        o_mb = _moba_attention(seq3(p["mb_q"]), seq3(p["mb_k"]), seq3(p["mb_v"]), kmean, gates)
        o_mem = _mem_attention(seq3(p["mem_q"]), mk, mv, gates, layer)
        mixed = [o.reshape(rows, o.shape[-1]) for o in (o_sb, o_mb, o_mem)]
        if layer + 1 < DEPTH:
            x2d, *proj = _next_projection(x2d, *mixed, w_out_b[layer], norm_g[layer + 1:layer + 2],
                                          w_in_b[layer + 1], rope, seq)
        else:
            out = _final_projection(x2d, *mixed, w_out_b[layer], final_norm_g.reshape(1, D_MODEL))
    return out.reshape(batch, seq, D_MODEL)
```

```python
import functools

import numpy as np
import jax
import jax.numpy as jnp
from jax import lax
from jax.experimental import pallas as pl
from jax.experimental.pallas import tpu as pltpu

D_MODEL = 1024
DEPTH = 2
HEAD_DIM = 64
MEM_HEADS = 4
SB_HEADS = 6
MOBA_HEADS = 6
MEM_WIDTH = MEM_HEADS * HEAD_DIM
SB_WIDTH = SB_HEADS * HEAD_DIM
MOBA_WIDTH = MOBA_HEADS * HEAD_DIM
MIX_WIDTH = SB_WIDTH + MOBA_WIDTH + MEM_WIDTH
IN_COLS = 4 * SB_WIDTH + 4 * MOBA_WIDTH + 2 * MEM_WIDTH
MOBA_BLOCK = 256
MOBA_TOPK = 3
ROPE_THETA = 500000.0
ROPE_DIMS = HEAD_DIM // 4
ROPE_HALF = ROPE_DIMS // 2
NORM_EPS = 1e-6
NEG_INF = -1e30
QK_SCALE = HEAD_DIM ** -0.5
LOG2_E = 1.4426950408889634

LANES = 128
V7X_VMEM_LIMIT = 48 * 1024 * 1024

PAIR = LANES
PROJ_ROWS = 512
PROJ_CHUNK = 512
Q_ROWS = 256
KEY_BLOCK = 256
CUM_BLOCK = 256
MEM_Q_ROWS = 1024

F32 = jnp.float32
BF16 = jnp.bfloat16
_NT = (((1,), (1,)), ((), ()))

_GROUPS = (("sb_q", SB_WIDTH), ("sb_k", SB_WIDTH), ("sb_v", SB_WIDTH), ("sb_g", SB_WIDTH),
           ("mb_q", MOBA_WIDTH), ("mb_k", MOBA_WIDTH), ("mb_v", MOBA_WIDTH), ("mb_g", MOBA_WIDTH),
           ("mem_q", MEM_WIDTH), ("mem_g", MEM_WIDTH))
_GATE_OFFSET = {"sb_g": 0, "mb_g": SB_WIDTH, "mem_g": SB_WIDTH + MOBA_WIDTH}


def _unit_table():
    units = []
    for name, width in _GROUPS:
        for off in range(0, width, LANES):
            units.append((name, off))
    return units


_UNITS = _unit_table()


def _rms_norm(x, gain):
    ms = jnp.mean(x * x, axis=-1, keepdims=True)
    return x * lax.rsqrt(ms + NORM_EPS) * gain


def _rotary(y, cos, sin_up, sin_dn):
    return y * cos + pltpu.roll(y, ROPE_HALF, 1) * sin_up + pltpu.roll(y, LANES - ROPE_HALF, 1) * sin_dn


def _project(x, gain_ref, w_ref, cos_ref, sup_ref, sdn_ref, outs):
    h = _rms_norm(x, gain_ref[...]).astype(BF16)
    cos, sup, sdn = cos_ref[...], sup_ref[...], sdn_ref[...]
    units_per_chunk = PROJ_CHUNK // LANES
    for c in range(IN_COLS // PROJ_CHUNK):
        y = jnp.dot(h, w_ref[:, c * PROJ_CHUNK:(c + 1) * PROJ_CHUNK], preferred_element_type=F32)
        for u in range(units_per_chunk):
            name, off = _UNITS[c * units_per_chunk + u]
            piece = y[:, u * LANES:(u + 1) * LANES]
            if name in _GATE_OFFSET:
                g0 = _GATE_OFFSET[name] + off
                outs["gates"][:, g0:g0 + LANES] = piece
                continue
            if name in ("mb_q", "mb_k"):
                piece = _rotary(piece, cos, sup, sdn)
            if name == "mb_k":
                for blk in range(PROJ_ROWS // MOBA_BLOCK):
                    rows = piece[blk * MOBA_BLOCK:(blk + 1) * MOBA_BLOCK]
                    outs["kmean"][0, blk:blk + 1, off:off + LANES] = jnp.mean(rows, axis=0, keepdims=True)
            if name.endswith("_q"):
                piece = piece * QK_SCALE
            outs[name][:, off:off + LANES] = piece.astype(BF16)


_PROJ_OUT_NAMES = ("sb_q", "sb_k", "sb_v", "mb_q", "mb_k", "mb_v", "mem_q", "gates", "kmean")


def _first_proj_kernel(x_ref, gain_ref, w_ref, cos_ref, sup_ref, sdn_ref, *out_refs):
    outs = dict(zip(_PROJ_OUT_NAMES, out_refs))
    _project(x_ref[...], gain_ref, w_ref, cos_ref, sup_ref, sdn_ref, outs)


def _residual(x_ref, osb_ref, omb_ref, omem_ref, wo_ref):
    mixed = jnp.concatenate([osb_ref[...], omb_ref[...], omem_ref[...]], axis=1)
    return x_ref[...] + jnp.dot(mixed, wo_ref[...], preferred_element_type=F32)


def _next_proj_kernel(x_ref, osb_ref, omb_ref, omem_ref, wo_ref, gain_ref, w_ref, cos_ref, sup_ref, sdn_ref,
                      xnew_ref, *out_refs):
    outs = dict(zip(_PROJ_OUT_NAMES, out_refs))
    xnew = _residual(x_ref, osb_ref, omb_ref, omem_ref, wo_ref)
    xnew_ref[...] = xnew
    _project(xnew, gain_ref, w_ref, cos_ref, sup_ref, sdn_ref, outs)


def _final_kernel(x_ref, osb_ref, omb_ref, omem_ref, wo_ref, gain_ref, out_ref):
    out_ref[...] = _rms_norm(_residual(x_ref, osb_ref, omb_ref, omem_ref, wo_ref), gain_ref[...])


def _row_spec(cols):
    return pl.BlockSpec((PROJ_ROWS, cols), lambda r: (r, 0))


def _const_spec(shape):
    return pl.BlockSpec(shape, lambda r: (0,) * len(shape))


def _proj_out(rows):
    shapes = {"sb_q": SB_WIDTH, "sb_k": SB_WIDTH, "sb_v": SB_WIDTH, "mb_q": MOBA_WIDTH, "mb_k": MOBA_WIDTH,
              "mb_v": MOBA_WIDTH, "mem_q": MEM_WIDTH}
    out_shape, out_specs = [], []
    for name in _PROJ_OUT_NAMES:
        if name == "gates":
            out_shape.append(jax.ShapeDtypeStruct((rows, MIX_WIDTH), F32))
            out_specs.append(_row_spec(MIX_WIDTH))
        elif name == "kmean":
            blocks = PROJ_ROWS // MOBA_BLOCK
            out_shape.append(jax.ShapeDtypeStruct((rows // PROJ_ROWS, blocks, MOBA_WIDTH), F32))
            out_specs.append(pl.BlockSpec((1, blocks, MOBA_WIDTH), lambda r: (r, 0, 0)))
        else:
            out_shape.append(jax.ShapeDtypeStruct((rows, shapes[name]), BF16))
            out_specs.append(_row_spec(shapes[name]))
    return out_shape, out_specs


def _rope_specs(seq):
    steps = seq // PROJ_ROWS
    return [pl.BlockSpec((PROJ_ROWS, LANES), lambda r: (r % steps, 0))] * 3


_PARAMS_1D = pltpu.CompilerParams(dimension_semantics=("arbitrary",), vmem_limit_bytes=V7X_VMEM_LIMIT)


def _first_projection(x2d, gain, w, rope, seq):
    rows = x2d.shape[0]
    out_shape, out_specs = _proj_out(rows)
    return pl.pallas_call(
        _first_proj_kernel,
        grid=(rows // PROJ_ROWS,),
        in_specs=[_row_spec(D_MODEL), _const_spec((1, D_MODEL)), _const_spec((D_MODEL, IN_COLS))] + _rope_specs(seq),
        out_specs=out_specs,
        out_shape=out_shape,
        compiler_params=_PARAMS_1D,
        name="first_projection",
    )(x2d, gain, w, *rope)


def _mix_specs():
    return [_row_spec(D_MODEL), _row_spec(SB_WIDTH), _row_spec(MOBA_WIDTH), _row_spec(MEM_WIDTH),
            _const_spec((MIX_WIDTH, D_MODEL))]


def _next_projection(x2d, osb, omb, omem, wo, gain, w, rope, seq):
    rows = x2d.shape[0]
    out_shape, out_specs = _proj_out(rows)
    return pl.pallas_call(
        _next_proj_kernel,
        grid=(rows // PROJ_ROWS,),
        in_specs=_mix_specs() + [_const_spec((1, D_MODEL)), _const_spec((D_MODEL, IN_COLS))] + _rope_specs(seq),
        out_specs=[_row_spec(D_MODEL)] + out_specs,
        out_shape=[jax.ShapeDtypeStruct((rows, D_MODEL), F32)] + out_shape,
        compiler_params=_PARAMS_1D,
        name="next_projection",
    )(x2d, osb, omb, omem, wo, gain, w, *rope)


def _final_projection(x2d, osb, omb, omem, wo, gain):
    rows = x2d.shape[0]
    return pl.pallas_call(
        _final_kernel,
        grid=(rows // PROJ_ROWS,),
        in_specs=_mix_specs() + [_const_spec((1, D_MODEL))],
        out_specs=_row_spec(D_MODEL),
        out_shape=jax.ShapeDtypeStruct((rows, D_MODEL), F32),
        compiler_params=_PARAMS_1D,
        name="final_projection",
    )(x2d, osb, omb, omem, wo, gain)


def _mem_kv_kernel(mem_ref, gain_ref, w_ref, mk_ref, mv_ref):
    m = _rms_norm(mem_ref[0], gain_ref[0]).astype(BF16)
    kv = jnp.dot(m, w_ref[0], preferred_element_type=F32)
    mk_ref[0, 0] = kv[:, :MEM_WIDTH].astype(BF16)
    mv_ref[0, 0] = kv[:, MEM_WIDTH:].astype(BF16)


def _mem_kv(mem, gains, w):
    batch, mem_len, _ = mem.shape
    out = jax.ShapeDtypeStruct((DEPTH, batch, mem_len, MEM_WIDTH), BF16)
    out_spec = pl.BlockSpec((1, 1, mem_len, MEM_WIDTH), lambda l, b: (l, b, 0, 0))
    return pl.pallas_call(
        _mem_kv_kernel,
        grid=(DEPTH, batch),
        in_specs=[pl.BlockSpec((1, mem_len, D_MODEL), lambda l, b: (b, 0, 0)),
                  pl.BlockSpec((1, 1, D_MODEL), lambda l, b: (l, 0, 0)),
                  pl.BlockSpec((1, D_MODEL, 2 * MEM_WIDTH), lambda l, b: (l, 0, 0))],
        out_specs=[out_spec, out_spec],
        out_shape=[out, out],
        compiler_params=pltpu.CompilerParams(dimension_semantics=("arbitrary", "arbitrary"),
                                             vmem_limit_bytes=V7X_VMEM_LIMIT),
        name="memory_kv",
    )(mem, gains, w)


def _split_heads(q_pair):
    lane = lax.broadcasted_iota(jnp.int32, q_pair.shape, 1)
    zero = jnp.zeros_like(q_pair)
    return jnp.where(lane < HEAD_DIM, q_pair, zero), jnp.where(lane >= HEAD_DIM, q_pair, zero)


def _merge_heads(even, odd):
    lane = lax.broadcasted_iota(jnp.int32, even.shape, 1)
    return jnp.where(lane < HEAD_DIM, even, odd)


def _silu(g):
    return g * jax.nn.sigmoid(g)


def _local_positions(keys):
    row = lax.broadcasted_iota(jnp.int32, (Q_ROWS, keys), 0)
    col = lax.broadcasted_iota(jnp.int32, (Q_ROWS, keys), 1)
    return row, col


def _for_query_block_pairs(step, n_blocks, block):
    def both(c):
        block(c)
        block(n_blocks - 1 - c)

    del step
    for c in range(n_blocks // 2):
        both(c)


def _softplus2(z2):
    return jnp.maximum(z2, 0.0) + jnp.log2(1.0 + jnp.exp2(-jnp.abs(z2)))


def _sb_head(q, k, v, neg_cum, diagonal_start, row, col):
    chunks = k.shape[0] // CUM_BLOCK
    z2 = lax.dot_general(q, k, _NT, preferred_element_type=F32) * LOG2_E
    pieces, totals, past = [], [], {}
    for c in range(chunks):
        soft = _softplus2(z2[:, c * CUM_BLOCK:(c + 1) * CUM_BLOCK])
        if c * CUM_BLOCK >= diagonal_start:
            past[c] = col + (c * CUM_BLOCK - diagonal_start) < row
            soft = jnp.where(past[c], soft, 0.0)
        pieces.append(soft.astype(BF16))
        totals.append(jnp.sum(soft, axis=-1, keepdims=True))
    sums = jnp.dot(jnp.concatenate(pieces, axis=0), neg_cum, preferred_element_type=F32)
    tail = None
    acc = None
    for c in reversed(range(chunks)):
        e = z2[:, c * CUM_BLOCK:(c + 1) * CUM_BLOCK] + sums[c * Q_ROWS:(c + 1) * Q_ROWS]
        if tail is not None:
            e = e - tail
        w = jnp.exp2(e)
        if c in past:
            w = jnp.where(past[c], w, 0.0)
        pv = jnp.dot(w.astype(BF16), v[c * CUM_BLOCK:(c + 1) * CUM_BLOCK], preferred_element_type=F32)
        acc = pv if acc is None else acc + pv
        tail = totals[c] if tail is None else tail + totals[c]
    return acc


def _sb_kernel(q_ref, k_ref, v_ref, g_ref, cum_ref, o_ref):
    row, col = _local_positions(CUM_BLOCK)

    def block(n):
        rows = slice(n * Q_ROWS, (n + 1) * Q_ROWS)
        keys = (n + 1) * KEY_BLOCK
        acc = [_sb_head(q, k_ref[0, :keys, :], v_ref[0, :keys, :], cum_ref[...], n * KEY_BLOCK, row, col)
               for q in _split_heads(q_ref[0, rows, :])]
        o_ref[0, rows, :] = (_merge_heads(*acc) * _silu(g_ref[0, rows, :])).astype(BF16)

    _for_query_block_pairs(pl.program_id(2), k_ref.shape[1] // KEY_BLOCK, block)


def _softmax_pv(blocks, v_ones):
    m = functools.reduce(jnp.maximum, blocks)
    m = jnp.max(jnp.maximum(m[:, :LANES], m[:, LANES:]), axis=-1, keepdims=True)
    p = jnp.concatenate([jnp.exp(b - m).astype(BF16) for b in blocks], axis=1)
    acc = jnp.dot(p, v_ones, preferred_element_type=F32)
    return acc[:, :PAIR] / acc[:, PAIR:]


def _unselected(q, kmean, n):
    blocks = kmean.shape[0]
    km_hi = kmean.astype(BF16).astype(F32)
    pieces = jnp.concatenate([km_hi, kmean - km_hi], axis=0).astype(BF16)
    g2 = lax.dot_general(pieces, q, _NT, preferred_element_type=F32)
    gate = g2[:blocks] + g2[blocks:]
    blk = lax.broadcasted_iota(jnp.int32, gate.shape, 0)
    gate = jnp.where(blk < n, gate, -jnp.inf)
    ahead = jnp.zeros_like(gate)
    for j in range(n):
        gj = gate[j:j + 1, :]
        ahead = ahead + jnp.where((gj > gate) | ((gj == gate) & (blk > j)), 1.0, 0.0)
    unsel = jnp.where((ahead >= MOBA_TOPK) & (blk < n), 1.0, 0.0)
    unsel = jnp.concatenate([unsel, jnp.zeros((LANES - blocks, unsel.shape[1]), F32)], axis=0)
    return unsel.T.astype(BF16)


def _moba_kernel(q_ref, k_ref, v_ref, kmean_ref, ind_ref, g_ref, o_ref, kaug_ref, vaug_ref):
    step = pl.program_id(2)

    @pl.when(step == 0)
    def _():
        kaug_ref[:, :PAIR] = k_ref[0]
        kaug_ref[:, PAIR:] = ind_ref[...]
        vaug_ref[:, :PAIR] = v_ref[0]
        vaug_ref[:, PAIR:] = jnp.ones(v_ref.shape[1:], BF16)

    row, col = _local_positions(KEY_BLOCK)

    def head(q, n):
        keys = (n + 1) * KEY_BLOCK
        if n <= MOBA_TOPK:
            s = lax.dot_general(q, k_ref[0, :keys, :], _NT, preferred_element_type=F32)
        else:
            lhs = jnp.concatenate([q, _unselected(q, kmean_ref[0], n)], axis=1)
            s = lax.dot_general(lhs, kaug_ref[:keys, :], _NT, preferred_element_type=F32)
        own = jnp.where(col <= row, s[:, n * KEY_BLOCK:], NEG_INF)
        blocks = [s[:, j * KEY_BLOCK:(j + 1) * KEY_BLOCK] for j in range(n)] + [own]
        return _softmax_pv(blocks, vaug_ref[:keys, :])

    def block(n):
        rows = slice(n * Q_ROWS, (n + 1) * Q_ROWS)
        out = [head(q, n) for q in _split_heads(q_ref[0, rows, :])]
        o_ref[0, rows, :] = (_merge_heads(*out) * _silu(g_ref[0, rows, :])).astype(BF16)

    _for_query_block_pairs(step, k_ref.shape[1] // KEY_BLOCK, block)


def _mem_kernel(q_ref, mk_ref, mv_ref, g_ref, o_ref):
    mv = mv_ref[0, 0]
    v_ones = jnp.concatenate([mv, jnp.ones_like(mv)], axis=1)
    for r in range(MEM_Q_ROWS // Q_ROWS):
        rows = slice(r * Q_ROWS, (r + 1) * Q_ROWS)
        out = [_softmax_pv([lax.dot_general(q, mk_ref[0, 0], _NT, preferred_element_type=F32)], v_ones)
               for q in _split_heads(q_ref[0, rows, :])]
        o_ref[0, rows, :] = (_merge_heads(*out) * _silu(g_ref[0, rows, :])).astype(BF16)


_PARAMS_3D = pltpu.CompilerParams(dimension_semantics=("arbitrary", "arbitrary", "arbitrary"),
                                  vmem_limit_bytes=V7X_VMEM_LIMIT)


def _q_spec(col_block_offset=0, rows=Q_ROWS):
    return pl.BlockSpec((1, rows, PAIR), lambda b, p, i: (b, i, p + col_block_offset))


def _seq_spec(seq, col_block_offset=0):
    return pl.BlockSpec((1, seq, PAIR), lambda b, p, i: (b, 0, p + col_block_offset))


def _neg_cum_matrix():
    j = np.arange(CUM_BLOCK)
    return jnp.asarray(-(j[:, None] >= j[None, :]).astype(np.float32), dtype=BF16)


def _block_indicator(seq):
    s = np.arange(seq)[:, None] // MOBA_BLOCK
    return jnp.asarray(np.where(s == np.arange(LANES)[None, :], NEG_INF, 0.0), dtype=BF16)


def _sb_attention(q, k, v, gates):
    batch, seq, width = q.shape
    grid = (batch, width // PAIR, 1)
    return pl.pallas_call(
        _sb_kernel,
        grid=grid,
        in_specs=[_seq_spec(seq), _seq_spec(seq), _seq_spec(seq), _seq_spec(seq, _GATE_OFFSET["sb_g"] // PAIR),
                  pl.BlockSpec((CUM_BLOCK, CUM_BLOCK), lambda b, p, i: (0, 0))],
        out_specs=_seq_spec(seq),
        out_shape=jax.ShapeDtypeStruct(q.shape, BF16),
        compiler_params=_PARAMS_3D,
        name="stick_breaking_attention",
    )(q, k, v, gates, _neg_cum_matrix())


def _moba_attention(q, k, v, kmean, gates):
    batch, seq, width = q.shape
    grid = (batch, width // PAIR, 1)
    return pl.pallas_call(
        _moba_kernel,
        grid=grid,
        in_specs=[_seq_spec(seq), _seq_spec(seq), _seq_spec(seq),
                  pl.BlockSpec((1, seq // MOBA_BLOCK, PAIR), lambda b, p, i: (b, 0, p)),
                  pl.BlockSpec((seq, LANES), lambda b, p, i: (0, 0)),
                  _seq_spec(seq, _GATE_OFFSET["mb_g"] // PAIR)],
        out_specs=_seq_spec(seq),
        out_shape=jax.ShapeDtypeStruct(q.shape, BF16),
        scratch_shapes=[pltpu.VMEM((seq, 2 * PAIR), BF16), pltpu.VMEM((seq, 2 * PAIR), BF16)],
        compiler_params=_PARAMS_3D,
        name="moba_attention",
    )(q, k, v, kmean, _block_indicator(seq), gates)


def _mem_attention(q, mk, mv, gates, layer):
    batch, seq, width = q.shape
    mem_len = mk.shape[2]
    grid = (batch, width // PAIR, seq // MEM_Q_ROWS)
    mem_spec = pl.BlockSpec((1, 1, mem_len, PAIR), lambda b, p, i: (layer, b, 0, p))
    return pl.pallas_call(
        _mem_kernel,
        grid=grid,
        in_specs=[_q_spec(rows=MEM_Q_ROWS), mem_spec, mem_spec,
                  _q_spec(_GATE_OFFSET["mem_g"] // PAIR, rows=MEM_Q_ROWS)],
        out_specs=_q_spec(rows=MEM_Q_ROWS),
        out_shape=jax.ShapeDtypeStruct(q.shape, BF16),
        compiler_params=_PARAMS_3D,
        name="memory_attention",
    )(q, mk, mv, gates)


def _rope_tables(seq):
    inv_freq = jnp.float32(ROPE_THETA) ** (-jnp.arange(ROPE_HALF, dtype=F32) * 2.0 / ROPE_DIMS)
    ang = jnp.arange(seq).astype(F32)[:, None] * inv_freq[None, :]
    cos, sin = jnp.cos(ang), jnp.sin(ang)
    d = np.arange(LANES) % HEAD_DIM
    f = d % ROPE_HALF
    cos_t = jnp.where(d < ROPE_DIMS, cos[:, f], 1.0)
    sin_up = jnp.where((d >= ROPE_HALF) & (d < ROPE_DIMS), sin[:, f], 0.0)
    sin_dn = jnp.where(d < ROPE_HALF, -sin[:, f], 0.0)
    return cos_t, sin_up, sin_dn


def kernel(x, mem, norm_g, w_in, mem_norm_g, w_mem_kv, w_out, final_norm_g):
    batch, seq, d_model = x.shape
    assert d_model == D_MODEL and seq % PROJ_ROWS == 0 and seq % MEM_Q_ROWS == 0
    assert seq // MOBA_BLOCK <= LANES
    rows = batch * seq
    w_in_b, w_out_b, w_kv_b = w_in.astype(BF16), w_out.astype(BF16), w_mem_kv.astype(BF16)
    rope = _rope_tables(seq)
    mk, mv = _mem_kv(mem, mem_norm_g.reshape(DEPTH, 1, D_MODEL), w_kv_b)

    x2d = x.reshape(rows, D_MODEL)
    proj = _first_projection(x2d, norm_g[0:1], w_in_b[0], rope, seq)
    for layer in range(DEPTH):
        p = dict(zip(_PROJ_OUT_NAMES, proj))
        seq3 = lambda a: a.reshape(batch, seq, a.shape[-1])
        gates = seq3(p["gates"])
        o_sb = _sb_attention(seq3(p["sb_q"]), seq3(p["sb_k"]), seq3(p["sb_v"]), gates)
        kmean = p["kmean"].reshape(batch, seq // MOBA_BLOCK, MOBA_WIDTH)
        o_mb = _moba_attention(seq3(p["mb_q"]), seq3(p["mb_k"]), seq3(p["mb_v"]), kmean, gates)
        o_mem = _mem_attention(seq3(p["mem_q"]), mk, mv, gates, layer)
        mixed = [o.reshape(rows, o.shape[-1]) for o in (o_sb, o_mb, o_mem)]
        if layer + 1 < DEPTH:
            x2d, *proj = _next_projection(x2d, *mixed, w_out_b[layer], norm_g[layer + 1:layer + 2],
                                          w_in_b[layer + 1], rope, seq)
        else:
            out = _final_projection(x2d, *mixed, w_out_b[layer], final_norm_g.reshape(1, D_MODEL))
    return out.reshape(batch, seq, D_MODEL)
```

```python
import functools

import numpy as np
import jax
import jax.numpy as jnp
from jax import lax
from jax.experimental import pallas as pl
from jax.experimental.pallas import tpu as pltpu

D_MODEL = 1024
DEPTH = 2
HEAD_DIM = 64
MEM_HEADS = 4
SB_HEADS = 6
MOBA_HEADS = 6
MEM_WIDTH = MEM_HEADS * HEAD_DIM
SB_WIDTH = SB_HEADS * HEAD_DIM
MOBA_WIDTH = MOBA_HEADS * HEAD_DIM
MIX_WIDTH = SB_WIDTH + MOBA_WIDTH + MEM_WIDTH
IN_COLS = 4 * SB_WIDTH + 4 * MOBA_WIDTH + 2 * MEM_WIDTH
MOBA_BLOCK = 256
MOBA_TOPK = 3
ROPE_THETA = 500000.0
ROPE_DIMS = HEAD_DIM // 4
ROPE_HALF = ROPE_DIMS // 2
NORM_EPS = 1e-6
NEG_INF = -1e30
QK_SCALE = HEAD_DIM ** -0.5
LOG2_E = 1.4426950408889634

LANES = 128
V7X_VMEM_LIMIT = 56 * 1024 * 1024

PAIR = LANES
PROJ_ROWS = 512
PROJ_CHUNK = 512
Q_ROWS = 256
KEY_BLOCK = 256
CUM_BLOCK = 256

F32 = jnp.float32
BF16 = jnp.bfloat16
_NT = (((1,), (1,)), ((), ()))

_GROUPS = (("sb_q", SB_WIDTH), ("sb_k", SB_WIDTH), ("sb_v", SB_WIDTH), ("sb_g", SB_WIDTH),
           ("mb_q", MOBA_WIDTH), ("mb_k", MOBA_WIDTH), ("mb_v", MOBA_WIDTH), ("mb_g", MOBA_WIDTH),
           ("mem_q", MEM_WIDTH), ("mem_g", MEM_WIDTH))
_GATE_OFFSET = {"sb_g": 0, "mb_g": SB_WIDTH, "mem_g": SB_WIDTH + MOBA_WIDTH}


def _unit_table():
    units = []
    for name, width in _GROUPS:
        for off in range(0, width, LANES):
            units.append((name, off))
    return units


_UNITS = _unit_table()


def _rms_norm(x, gain):
    ms = jnp.mean(x * x, axis=-1, keepdims=True)
    return x * lax.rsqrt(ms + NORM_EPS) * gain


def _rotary(y, cos, sin_up, sin_dn):
    return y * cos + pltpu.roll(y, ROPE_HALF, 1) * sin_up + pltpu.roll(y, LANES - ROPE_HALF, 1) * sin_dn


def _project(x, gain_ref, w_ref, cos_ref, sup_ref, sdn_ref, outs):
    h = _rms_norm(x, gain_ref[...]).astype(BF16)
    cos, sup, sdn = cos_ref[...], sup_ref[...], sdn_ref[...]
    units_per_chunk = PROJ_CHUNK // LANES
    for c in range(IN_COLS // PROJ_CHUNK):
        y = jnp.dot(h, w_ref[:, c * PROJ_CHUNK:(c + 1) * PROJ_CHUNK], preferred_element_type=F32)
        for u in range(units_per_chunk):
            name, off = _UNITS[c * units_per_chunk + u]
            piece = y[:, u * LANES:(u + 1) * LANES]
            if name in _GATE_OFFSET:
                g0 = _GATE_OFFSET[name] + off
                outs["gates"][:, g0:g0 + LANES] = piece
                continue
            if name in ("mb_q", "mb_k"):
                piece = _rotary(piece, cos, sup, sdn)
            if name == "mb_k":
                for blk in range(PROJ_ROWS // MOBA_BLOCK):
                    rows = piece[blk * MOBA_BLOCK:(blk + 1) * MOBA_BLOCK]
                    outs["kmean"][0, blk:blk + 1, off:off + LANES] = jnp.mean(rows, axis=0, keepdims=True)
            if name.endswith("_q"):
                piece = piece * (QK_SCALE * LOG2_E)
            outs[name][:, off:off + LANES] = piece.astype(BF16)


_PROJ_OUT_NAMES = ("sb_q", "sb_k", "sb_v", "mb_q", "mb_k", "mb_v", "mem_q", "gates", "kmean")


def _cast_weight_once(w32_ref, w_ref):
    @pl.when(pl.program_id(0) == 0)
    def _():
        cols = w_ref.shape[1]
        chunk = min(cols, PROJ_CHUNK)
        for c in range(cols // chunk):
            w_ref[:, c * chunk:(c + 1) * chunk] = w32_ref[0, :, c * chunk:(c + 1) * chunk].astype(BF16)


def _first_proj_kernel(x_ref, gain_ref, w32_ref, cos_ref, sup_ref, sdn_ref, *rest):
    out_refs, (w_ref,) = rest[:len(_PROJ_OUT_NAMES)], rest[len(_PROJ_OUT_NAMES):]
    _cast_weight_once(w32_ref, w_ref)
    outs = dict(zip(_PROJ_OUT_NAMES, out_refs))
    _project(x_ref[...], gain_ref, w_ref, cos_ref, sup_ref, sdn_ref, outs)


def _memory_attention(q_ref, mk_ref, mv_ref, g_ref):
    outs = []
    for p in range(MEM_WIDTH // PAIR):
        cols = slice(p * PAIR, (p + 1) * PAIR)
        mv = mv_ref[0, 0, :, cols]
        v_ones = jnp.concatenate([mv, jnp.ones_like(mv)], axis=1)
        s = lax.dot_general(_stack_heads(q_ref[:, cols]), mk_ref[0, 0, :, cols], _NT, preferred_element_type=F32)
        outs.append((_unstack_heads(_softmax_pv([s], v_ones)) * _silu(g_ref[:, cols])).astype(BF16))
    return jnp.concatenate(outs, axis=1)


def _residual(x_ref, osb_ref, omb_ref, memq_ref, mk_ref, mv_ref, memg_ref, wo_ref):
    o_mem = _memory_attention(memq_ref, mk_ref, mv_ref, memg_ref)
    mixed = jnp.concatenate([osb_ref[...], omb_ref[...], o_mem], axis=1)
    return x_ref[...] + jnp.dot(mixed, wo_ref[...], preferred_element_type=F32)


_MIX_REFS = 8


def _next_proj_kernel(*refs):
    mix, (gain_ref, w32_ref, cos_ref, sup_ref, sdn_ref, xnew_ref) = refs[:_MIX_REFS], refs[_MIX_REFS:_MIX_REFS + 6]
    out_refs, (w_ref, wo_ref) = refs[_MIX_REFS + 6:-2], refs[-2:]
    _cast_weight_once(w32_ref, w_ref)
    _cast_weight_once(mix[-1], wo_ref)
    outs = dict(zip(_PROJ_OUT_NAMES, out_refs))
    xnew = _residual(*mix[:-1], wo_ref)
    xnew_ref[...] = xnew
    _project(xnew, gain_ref, w_ref, cos_ref, sup_ref, sdn_ref, outs)


def _final_kernel(*refs):
    mix, (gain_ref, out_ref, wo_ref) = refs[:_MIX_REFS], refs[_MIX_REFS:]
    _cast_weight_once(mix[-1], wo_ref)
    out_ref[...] = _rms_norm(_residual(*mix[:-1], wo_ref), gain_ref[...])


def _row_spec(cols):
    return pl.BlockSpec((PROJ_ROWS, cols), lambda r: (r, 0))


def _const_spec(shape):
    return pl.BlockSpec(shape, lambda r: (0,) * len(shape))


def _layer_spec(layer, shape):
    return pl.BlockSpec((1,) + shape, lambda r: (layer,) + (0,) * len(shape))


def _proj_out(rows):
    shapes = {"sb_q": SB_WIDTH, "sb_k": SB_WIDTH, "sb_v": SB_WIDTH, "mb_q": MOBA_WIDTH, "mb_k": MOBA_WIDTH,
              "mb_v": MOBA_WIDTH, "mem_q": MEM_WIDTH}
    out_shape, out_specs = [], []
    for name in _PROJ_OUT_NAMES:
        if name == "gates":
            out_shape.append(jax.ShapeDtypeStruct((rows, MIX_WIDTH), F32))
            out_specs.append(_row_spec(MIX_WIDTH))
        elif name == "kmean":
            blocks = PROJ_ROWS // MOBA_BLOCK
            out_shape.append(jax.ShapeDtypeStruct((rows // PROJ_ROWS, blocks, MOBA_WIDTH), F32))
            out_specs.append(pl.BlockSpec((1, blocks, MOBA_WIDTH), lambda r: (r, 0, 0)))
        else:
            out_shape.append(jax.ShapeDtypeStruct((rows, shapes[name]), BF16))
            out_specs.append(_row_spec(shapes[name]))
    return out_shape, out_specs


def _rope_specs(seq):
    steps = seq // PROJ_ROWS
    return [pl.BlockSpec((PROJ_ROWS, LANES), lambda r: (r % steps, 0))] * 3


_PARAMS_1D = pltpu.CompilerParams(dimension_semantics=("arbitrary",), vmem_limit_bytes=V7X_VMEM_LIMIT)
_W_IN_SCRATCH = pltpu.VMEM((D_MODEL, IN_COLS), BF16)
_W_OUT_SCRATCH = pltpu.VMEM((MIX_WIDTH, D_MODEL), BF16)


def _first_projection(x2d, gain, w, rope, seq):
    rows = x2d.shape[0]
    out_shape, out_specs = _proj_out(rows)
    return pl.pallas_call(
        _first_proj_kernel,
        grid=(rows // PROJ_ROWS,),
        in_specs=[_row_spec(D_MODEL), _const_spec((1, D_MODEL)), _layer_spec(0, (D_MODEL, IN_COLS))]
        + _rope_specs(seq),
        out_specs=out_specs,
        out_shape=out_shape,
        scratch_shapes=[_W_IN_SCRATCH],
        compiler_params=_PARAMS_1D,
        name="first_projection",
    )(x2d, gain, w, *rope)


def _mix_specs(layer, seq, mem_len):
    steps_per_batch = seq // PROJ_ROWS
    mem_spec = pl.BlockSpec((1, 1, mem_len, MEM_WIDTH), lambda r: (layer, r // steps_per_batch, 0, 0))
    gate_spec = pl.BlockSpec((PROJ_ROWS, MEM_WIDTH), lambda r: (r, _GATE_OFFSET["mem_g"] // MEM_WIDTH))
    return [_row_spec(D_MODEL), _row_spec(SB_WIDTH), _row_spec(MOBA_WIDTH), _row_spec(MEM_WIDTH),
            mem_spec, mem_spec, gate_spec, _layer_spec(layer, (MIX_WIDTH, D_MODEL))]


def _next_projection(mix, layer, gain, w, rope, seq):
    rows = mix[0].shape[0]
    out_shape, out_specs = _proj_out(rows)
    return pl.pallas_call(
        _next_proj_kernel,
        grid=(rows // PROJ_ROWS,),
        in_specs=_mix_specs(layer, seq, mix[4].shape[2])
        + [_const_spec((1, D_MODEL)), _layer_spec(layer + 1, (D_MODEL, IN_COLS))] + _rope_specs(seq),
        out_specs=[_row_spec(D_MODEL)] + out_specs,
        out_shape=[jax.ShapeDtypeStruct((rows, D_MODEL), F32)] + out_shape,
        scratch_shapes=[_W_IN_SCRATCH, _W_OUT_SCRATCH],
        compiler_params=_PARAMS_1D,
        name="next_projection",
    )(*mix, gain, w, *rope)


def _final_projection(mix, layer, gain, seq):
    rows = mix[0].shape[0]
    return pl.pallas_call(
        _final_kernel,
        grid=(rows // PROJ_ROWS,),
        in_specs=_mix_specs(layer, seq, mix[4].shape[2]) + [_const_spec((1, D_MODEL))],
        out_specs=_row_spec(D_MODEL),
        out_shape=jax.ShapeDtypeStruct((rows, D_MODEL), F32),
        scratch_shapes=[_W_OUT_SCRATCH],
        compiler_params=_PARAMS_1D,
        name="final_projection",
    )(*mix, gain)


def _mem_kv_kernel(mem_ref, gain_ref, w_ref, mk_ref, mv_ref):
    batch, mem_len, _ = mem_ref.shape
    m = _rms_norm(mem_ref[...].reshape(batch * mem_len, D_MODEL), gain_ref[0]).astype(BF16)
    kv = jnp.dot(m, w_ref[0].astype(BF16), preferred_element_type=F32)
    mk_ref[0] =kv[:, :MEM_WIDTH].astype(BF16).reshape(batch, mem_len, MEM_WIDTH)
    mv_ref[0] = kv[:, MEM_WIDTH:].astype(BF16).reshape(batch, mem_len, MEM_WIDTH)


def _mem_kv(mem, gains, w):
    batch, mem_len, _ = mem.shape
    out = jax.ShapeDtypeStruct((DEPTH, batch, mem_len, MEM_WIDTH), BF16)
    out_spec = pl.BlockSpec((1, batch, mem_len, MEM_WIDTH), lambda l: (l, 0, 0, 0))
    return pl.pallas_call(
        _mem_kv_kernel,
        grid=(DEPTH,),
        in_specs=[pl.BlockSpec((batch, mem_len, D_MODEL), lambda l: (0, 0, 0)),
                  pl.BlockSpec((1, 1, D_MODEL), lambda l: (l, 0, 0)),
                  pl.BlockSpec((1, D_MODEL, 2 * MEM_WIDTH), lambda l: (l, 0, 0))],
        out_specs=[out_spec, out_spec],
        out_shape=[out, out],
        compiler_params=_PARAMS_1D,
        name="memory_kv",
    )(mem, gains, w)


def _stack_heads(q_pair):
    lane = lax.broadcasted_iota(jnp.int32, q_pair.shape, 1)
    zero = jnp.zeros_like(q_pair)
    return jnp.concatenate([jnp.where(lane < HEAD_DIM, q_pair, zero), jnp.where(lane >= HEAD_DIM, q_pair, zero)],
                           axis=0)


def _unstack_heads(stacked):
    rows = stacked.shape[0] // 2
    lane = lax.broadcasted_iota(jnp.int32, (rows, stacked.shape[1]), 1)
    return jnp.where(lane < HEAD_DIM, stacked[:rows], stacked[rows:])


def _silu(g):
    return g * jax.nn.sigmoid(g)


def _local_positions(keys):
    row = lax.broadcasted_iota(jnp.int32, (2 * Q_ROWS, keys), 0) % Q_ROWS
    col = lax.broadcasted_iota(jnp.int32, (2 * Q_ROWS, keys), 1)
    return row, col


def _query_block_order(n_blocks):
    return list(reversed(range(n_blocks)))


def _softplus2(z2):
    return jnp.maximum(z2, 0.0) + jnp.log2(1.0 + jnp.exp2(-jnp.abs(z2)))


def _sb_head(q, k_reversed, v_reversed, neg_cum, row, col):
    chunks = k_reversed.shape[0] // CUM_BLOCK
    rows = q.shape[0]
    z2 = lax.dot_general(q, k_reversed, _NT, preferred_element_type=F32)
    past = col < row
    pieces, totals = [], []
    for c in range(chunks):
        soft = _softplus2(z2[:, c * CUM_BLOCK:(c + 1) * CUM_BLOCK])
        if c == 0:
            soft = jnp.where(past, soft, 0.0)
        pieces.append(soft.astype(BF16))
        totals.append(jnp.sum(soft, axis=-1, keepdims=True))
    sums = jnp.dot(jnp.concatenate(pieces, axis=0), neg_cum, preferred_element_type=F32)
    tail = None
    weights = []
    for c in range(chunks):
        e = z2[:, c * CUM_BLOCK:(c + 1) * CUM_BLOCK] + sums[c * rows:(c + 1) * rows]
        if c == 0:
            w = jnp.where(past, jnp.exp2(e), 0.0)
        else:
            w = jnp.exp2(e - tail)
        weights.append(w.astype(BF16))
        tail = totals[c] if tail is None else tail + totals[c]
    return jnp.dot(jnp.concatenate(weights, axis=1), v_reversed, preferred_element_type=F32)


def _sb_prepare(k_ref, v_ref, krev_ref, vrev_ref):
    seq = v_ref.shape[1]
    for c in range(seq // CUM_BLOCK):
        src = slice(c * CUM_BLOCK, (c + 1) * CUM_BLOCK)
        dst = slice(seq - (c + 1) * CUM_BLOCK, seq - c * CUM_BLOCK)
        krev_ref[dst, :] = k_ref[0, src, :]
        vrev_ref[dst, :] = v_ref[0, src, :]


def _sb_block(n, q_ref, g_ref, cum_ref, o_ref, krev_ref, vrev_ref):
    row, col = _local_positions(CUM_BLOCK)
    seq = krev_ref.shape[0]
    rows = slice(n * Q_ROWS, (n + 1) * Q_ROWS)
    keys = (n + 1) * KEY_BLOCK
    acc = _sb_head(_stack_heads(q_ref[0, rows, :]), krev_ref[seq - keys:, :], vrev_ref[seq - keys:, :],
                   cum_ref[...], row, col)
    o_ref[0, rows, :] = (_unstack_heads(acc) * _silu(g_ref[0, rows, :])).astype(BF16)


def _softmax_pv(blocks, v_ones):
    m = functools.reduce(jnp.maximum, blocks)
    m = jnp.max(jnp.maximum(m[:, :LANES], m[:, LANES:]), axis=-1, keepdims=True)
    p = jnp.concatenate([jnp.exp2(b - m).astype(BF16) for b in blocks], axis=1)
    acc = jnp.dot(p, v_ones, preferred_element_type=F32)
    return acc[:, :PAIR] / acc[:, PAIR:]


def _unselected(q, kmean, n):
    blocks = kmean.shape[0]
    km_hi = kmean.astype(BF16).astype(F32)
    pieces = jnp.concatenate([km_hi, kmean - km_hi], axis=0).astype(BF16)
    g2 = lax.dot_general(pieces, q, _NT, preferred_element_type=F32)
    gate = g2[:blocks] + g2[blocks:]
    blk = lax.broadcasted_iota(jnp.int32, gate.shape, 0)
    gate = jnp.where(blk < n, gate, -jnp.inf)
    ahead = jnp.zeros_like(gate)
    for j in range(n):
        gj = gate[j:j + 1, :]
        ahead = ahead + jnp.where((gj > gate) | ((gj == gate) & (blk > j)), 1.0, 0.0)
    unsel = jnp.where((ahead >= MOBA_TOPK) & (blk < n), 1.0, 0.0)
    unsel = jnp.concatenate([unsel, jnp.zeros((LANES - blocks, unsel.shape[1]), F32)], axis=0)
    return unsel.T.astype(BF16)


def _moba_prepare(k_ref, v_ref, ind_ref, kaug_ref, vaug_ref):
    kaug_ref[:, :PAIR] = k_ref[0]
    kaug_ref[:, PAIR:] = ind_ref[...]
    vaug_ref[:, :PAIR] = v_ref[0]
    vaug_ref[:, PAIR:] = jnp.ones(v_ref.shape[1:], BF16)


def _moba_block(n, q_ref, k_ref, kmean_ref, g_ref, o_ref, kaug_ref, vaug_ref):
    row, col = _local_positions(KEY_BLOCK)
    rows = slice(n * Q_ROWS, (n + 1) * Q_ROWS)
    keys = (n + 1) * KEY_BLOCK

    def head(q):
        if n <= MOBA_TOPK:
            s = lax.dot_general(q, k_ref[0, :keys, :], _NT, preferred_element_type=F32)
        else:
            lhs = jnp.concatenate([q, _unselected(q, kmean_ref[0], n)], axis=1)
            s = lax.dot_general(lhs, kaug_ref[:keys, :], _NT, preferred_element_type=F32)
        own = jnp.where(col <= row, s[:, n * KEY_BLOCK:], NEG_INF)
        blocks = [s[:, j * KEY_BLOCK:(j + 1) * KEY_BLOCK] for j in range(n)] + [own]
        return _softmax_pv(blocks, vaug_ref[:keys, :])

    out = head(_stack_heads(q_ref[0, rows, :]))
    o_ref[0, rows, :] = (_unstack_heads(out) * _silu(g_ref[0, rows, :])).astype(BF16)


def _sb_moba_kernel(sq_ref, sk_ref, sv_ref, sg_ref, cum_ref, mq_ref, mk_ref, mv_ref, kmean_ref, ind_ref, mg_ref,
                    so_ref, mo_ref, krev_ref, vrev_ref, kaug_ref, vaug_ref):
    _sb_prepare(sk_ref, sv_ref, krev_ref, vrev_ref)
    _moba_prepare(mk_ref, mv_ref, ind_ref, kaug_ref, vaug_ref)
    n_blocks = sk_ref.shape[1] // KEY_BLOCK
    for n in _query_block_order(n_blocks):
        _sb_block(n, sq_ref, sg_ref, cum_ref, so_ref, krev_ref, vrev_ref)
        _moba_block(n, mq_ref, mk_ref, kmean_ref, mg_ref, mo_ref, kaug_ref, vaug_ref)


_PARAMS_3D = pltpu.CompilerParams(dimension_semantics=("arbitrary", "arbitrary", "arbitrary"),
                                  vmem_limit_bytes=V7X_VMEM_LIMIT)


def _seq_spec(seq, col_block_offset=0):
    return pl.BlockSpec((1, seq, PAIR), lambda b, p, i: (b, 0, p + col_block_offset))


def _neg_cum_matrix():
    j = np.arange(CUM_BLOCK)
    return jnp.asarray(-(j[:, None] >= j[None, :]).astype(np.float32), dtype=BF16)


def _block_indicator(seq):
    s = np.arange(seq)[:, None] // MOBA_BLOCK
    return jnp.asarray(np.where(s == np.arange(LANES)[None, :], NEG_INF, 0.0), dtype=BF16)


def _sb_moba_attention(sb_qkv, mb_qkv, kmean, gates):
    batch, seq, width = sb_qkv[0].shape
    assert SB_WIDTH == MOBA_WIDTH == width
    qkv_specs = [_seq_spec(seq)] * 3
    out = jax.ShapeDtypeStruct((batch, seq, width), BF16)
    return pl.pallas_call(
        _sb_moba_kernel,
        grid=(batch, width // PAIR, 1),
        in_specs=qkv_specs + [_seq_spec(seq, _GATE_OFFSET["sb_g"] // PAIR),
                              pl.BlockSpec((CUM_BLOCK, CUM_BLOCK), lambda b, p, i: (0, 0))]
        + qkv_specs + [pl.BlockSpec((1, seq // MOBA_BLOCK, PAIR), lambda b, p, i: (b, 0, p)),
                       pl.BlockSpec((seq, LANES), lambda b, p, i: (0, 0)),
                       _seq_spec(seq, _GATE_OFFSET["mb_g"] // PAIR)],
        out_specs=[_seq_spec(seq), _seq_spec(seq)],
        out_shape=[out, out],
        scratch_shapes=[pltpu.VMEM((seq, PAIR), BF16), pltpu.VMEM((seq, PAIR), BF16),
                        pltpu.VMEM((seq, 2 * PAIR), BF16), pltpu.VMEM((seq, 2 * PAIR), BF16)],
        compiler_params=_PARAMS_3D,
        name="sb_moba_attention",
    )(*sb_qkv, gates, _neg_cum_matrix(), *mb_qkv, kmean, _block_indicator(seq), gates)


def _rope_tables(seq):
    inv_freq = np.float32(ROPE_THETA) ** (-np.arange(ROPE_HALF, dtype=np.float32) * np.float32(2.0 / ROPE_DIMS))
    ang = np.arange(seq, dtype=np.float32)[:, None] * inv_freq[None, :].astype(np.float32)
    cos, sin = np.cos(ang).astype(np.float32), np.sin(ang).astype(np.float32)
    d = np.arange(LANES) % HEAD_DIM
    f = d % ROPE_HALF
    cos_t = np.where(d < ROPE_DIMS, cos[:, f], np.float32(1.0))
    sin_up = np.where((d >= ROPE_HALF) & (d < ROPE_DIMS), sin[:, f], np.float32(0.0))
    sin_dn = np.where(d < ROPE_HALF, -sin[:, f], np.float32(0.0))
    return tuple(jnp.asarray(t, dtype=F32) for t in (cos_t, sin_up, sin_dn))


def kernel(x, mem, norm_g, w_in, mem_norm_g, w_mem_kv, w_out, final_norm_g):
    batch, seq, d_model = x.shape
    assert d_model == D_MODEL and seq % PROJ_ROWS == 0 and seq % Q_ROWS == 0
    assert seq // MOBA_BLOCK <= LANES
    rows = batch * seq
    rope = _rope_tables(seq)
    mk, mv = _mem_kv(mem, mem_norm_g.reshape(DEPTH, 1, D_MODEL), w_mem_kv)

    x2d = x.reshape(rows, D_MODEL)
    proj = _first_projection(x2d, norm_g[0:1], w_in, rope, seq)
    for layer in range(DEPTH):
        p = dict(zip(_PROJ_OUT_NAMES, proj))
        seq3 = lambda a: a.reshape(batch, seq, a.shape[-1])
        gates = seq3(p["gates"])
        kmean = p["kmean"].reshape(batch, seq // MOBA_BLOCK, MOBA_WIDTH)
        o_sb, o_mb = _sb_moba_attention([seq3(p[name]) for name in ("sb_q", "sb_k", "sb_v")],
                                        [seq3(p[name]) for name in ("mb_q", "mb_k", "mb_v")], kmean, gates)
        mix = [x2d, o_sb.reshape(rows, SB_WIDTH), o_mb.reshape(rows, MOBA_WIDTH), p["mem_q"], mk, mv, p["gates"],
               w_out]
        if layer + 1 < DEPTH:
            x2d, *proj = _next_projection(mix, layer, norm_g[layer + 1:layer + 2], w_in, rope, seq)
        else:
            out = _final_projection(mix, layer, final_norm_g.reshape(1, D_MODEL), seq)
    return out.reshape(batch, seq, D_MODEL)
```

```python
import functools

import numpy as np
import jax
import jax.numpy as jnp
from jax import lax
from jax.experimental import pallas as pl
from jax.experimental.pallas import tpu as pltpu

D_MODEL = 1024
DEPTH = 2
HEAD_DIM = 64
MEM_HEADS = 4
SB_HEADS = 6
MOBA_HEADS = 6
MEM_WIDTH = MEM_HEADS * HEAD_DIM
SB_WIDTH = SB_HEADS * HEAD_DIM
MOBA_WIDTH = MOBA_HEADS * HEAD_DIM
MIX_WIDTH = SB_WIDTH + MOBA_WIDTH + MEM_WIDTH
IN_COLS = 4 * SB_WIDTH + 4 * MOBA_WIDTH + 2 * MEM_WIDTH
MOBA_BLOCK = 256
MOBA_TOPK = 3
ROPE_THETA = 500000.0
ROPE_DIMS = HEAD_DIM // 4
ROPE_HALF = ROPE_DIMS // 2
NORM_EPS = 1e-6
NEG_INF = -1e30
QK_SCALE = HEAD_DIM ** -0.5
LOG2_E = 1.4426950408889634

LANES = 128
V7X_VMEM_LIMIT = 56 * 1024 * 1024

PAIR = LANES
PROJ_ROWS = 512
PROJ_CHUNK = 512
Q_ROWS = 256
KEY_BLOCK = 256
CUM_BLOCK = 256

F32 = jnp.float32
BF16 = jnp.bfloat16
_NT = (((1,), (1,)), ((), ()))

_GROUPS = (("sb_q", SB_WIDTH), ("sb_k", SB_WIDTH), ("sb_v", SB_WIDTH), ("sb_g", SB_WIDTH),
           ("mb_q", MOBA_WIDTH), ("mb_k", MOBA_WIDTH), ("mb_v", MOBA_WIDTH), ("mb_g", MOBA_WIDTH),
           ("mem_q", MEM_WIDTH), ("mem_g", MEM_WIDTH))
_GATE_OFFSET = {"sb_g": 0, "mb_g": SB_WIDTH, "mem_g": SB_WIDTH + MOBA_WIDTH}


def _unit_table():
    units = []
    for name, width in _GROUPS:
        for off in range(0, width, LANES):
            units.append((name, off))
    return units


_UNITS = _unit_table()


def _rms_norm(x, gain):
    ms = jnp.mean(x * x, axis=-1, keepdims=True)
    return x * lax.rsqrt(ms + NORM_EPS) * gain


def _rotary(y, cos, sin_up, sin_dn):
    return y * cos + pltpu.roll(y, ROPE_HALF, 1) * sin_up + pltpu.roll(y, LANES - ROPE_HALF, 1) * sin_dn


def _project(x, gain_ref, w_ref, cos_ref, sup_ref, sdn_ref, outs):
    h = _rms_norm(x, gain_ref[...]).astype(BF16)
    cos, sup, sdn = cos_ref[...], sup_ref[...], sdn_ref[...]
    units_per_chunk = PROJ_CHUNK // LANES
    for c in range(IN_COLS // PROJ_CHUNK):
        y = jnp.dot(h, w_ref[:, c * PROJ_CHUNK:(c + 1) * PROJ_CHUNK], preferred_element_type=F32)
        for u in range(units_per_chunk):
            name, off = _UNITS[c * units_per_chunk + u]
            piece = y[:, u * LANES:(u + 1) * LANES]
            if name in _GATE_OFFSET:
                g0 = _GATE_OFFSET[name] + off
                outs["gates"][:, g0:g0 + LANES] = piece
                continue
            if name in ("mb_q", "mb_k"):
                piece = _rotary(piece, cos, sup, sdn)
            if name == "mb_k":
                for blk in range(PROJ_ROWS // MOBA_BLOCK):
                    rows = piece[blk * MOBA_BLOCK:(blk + 1) * MOBA_BLOCK]
                    outs["kmean"][0, blk:blk + 1, off:off + LANES] = jnp.mean(rows, axis=0, keepdims=True)
            if name.endswith("_q"):
                piece = piece * (QK_SCALE * LOG2_E)
            outs[name][:, off:off + LANES] = piece.astype(BF16)


_PROJ_OUT_NAMES = ("sb_q", "sb_k", "sb_v", "mb_q", "mb_k", "mb_v", "mem_q", "gates", "kmean")


def _cast_weight_once(w32_ref, w_ref):
    @pl.when(pl.program_id(0) == 0)
    def _():
        cols = w_ref.shape[1]
        chunk = min(cols, PROJ_CHUNK)
        for c in range(cols // chunk):
            w_ref[:, c * chunk:(c + 1) * chunk] = w32_ref[0, :, c * chunk:(c + 1) * chunk].astype(BF16)


def _first_proj_kernel(x_ref, gain_ref, w32_ref, cos_ref, sup_ref, sdn_ref, *rest):
    out_refs, (w_ref,) = rest[:len(_PROJ_OUT_NAMES)], rest[len(_PROJ_OUT_NAMES):]
    _cast_weight_once(w32_ref, w_ref)
    outs = dict(zip(_PROJ_OUT_NAMES, out_refs))
    _project(x_ref[...], gain_ref, w_ref, cos_ref, sup_ref, sdn_ref, outs)


def _memory_attention(q_ref, mk_ref, mv_ref, g_ref):
    outs = []
    for p in range(MEM_WIDTH // PAIR):
        cols = slice(p * PAIR, (p + 1) * PAIR)
        mv = mv_ref[0, 0, :, cols]
        v_ones = jnp.concatenate([mv, jnp.ones_like(mv)], axis=1)
        s = lax.dot_general(_stack_heads(q_ref[:, cols]), mk_ref[0, 0, :, cols], _NT, preferred_element_type=F32)
        outs.append((_unstack_heads(_softmax_pv([s], v_ones)) * _silu(g_ref[:, cols])).astype(BF16))
    return jnp.concatenate(outs, axis=1)


def _residual(x_ref, osb_ref, omb_ref, memq_ref, mk_ref, mv_ref, memg_ref, wo_ref):
    o_mem = _memory_attention(memq_ref, mk_ref, mv_ref, memg_ref)
    mixed = jnp.concatenate([osb_ref[...], omb_ref[...], o_mem], axis=1)
    return x_ref[...] + jnp.dot(mixed, wo_ref[...], preferred_element_type=F32)


_MIX_REFS = 8


def _next_proj_kernel(*refs):
    mix, (gain_ref, w32_ref, cos_ref, sup_ref, sdn_ref, xnew_ref) = refs[:_MIX_REFS], refs[_MIX_REFS:_MIX_REFS + 6]
    out_refs, (w_ref, wo_ref) = refs[_MIX_REFS + 6:-2], refs[-2:]
    _cast_weight_once(w32_ref, w_ref)
    _cast_weight_once(mix[-1], wo_ref)
    outs = dict(zip(_PROJ_OUT_NAMES, out_refs))
    xnew = _residual(*mix[:-1], wo_ref)
    xnew_ref[...] = xnew
    _project(xnew, gain_ref, w_ref, cos_ref, sup_ref, sdn_ref, outs)


def _final_kernel(*refs):
    mix, (gain_ref, out_ref, wo_ref) = refs[:_MIX_REFS], refs[_MIX_REFS:]
    _cast_weight_once(mix[-1], wo_ref)
    out_ref[...] = _rms_norm(_residual(*mix[:-1], wo_ref), gain_ref[...])


def _row_spec(cols):
    return pl.BlockSpec((PROJ_ROWS, cols), lambda r: (r, 0))


def _const_spec(shape):
    return pl.BlockSpec(shape, lambda r: (0,) * len(shape))


def _layer_spec(layer, shape):
    return pl.BlockSpec((1,) + shape, lambda r: (layer,) + (0,) * len(shape))


def _proj_out(rows):
    shapes = {"sb_q": SB_WIDTH, "sb_k": SB_WIDTH, "sb_v": SB_WIDTH, "mb_q": MOBA_WIDTH, "mb_k": MOBA_WIDTH,
              "mb_v": MOBA_WIDTH, "mem_q": MEM_WIDTH}
    out_shape, out_specs = [], []
    for name in _PROJ_OUT_NAMES:
        if name == "gates":
            out_shape.append(jax.ShapeDtypeStruct((rows, MIX_WIDTH), F32))
            out_specs.append(_row_spec(MIX_WIDTH))
        elif name == "kmean":
            blocks = PROJ_ROWS // MOBA_BLOCK
            out_shape.append(jax.ShapeDtypeStruct((rows // PROJ_ROWS, blocks, MOBA_WIDTH), F32))
            out_specs.append(pl.BlockSpec((1, blocks, MOBA_WIDTH), lambda r: (r, 0, 0)))
        else:
            out_shape.append(jax.ShapeDtypeStruct((rows, shapes[name]), BF16))
            out_specs.append(_row_spec(shapes[name]))
    return out_shape, out_specs


def _rope_specs(seq):
    steps = seq // PROJ_ROWS
    return [pl.BlockSpec((PROJ_ROWS, LANES), lambda r: (r % steps, 0))] * 3


_PARAMS_1D = pltpu.CompilerParams(dimension_semantics=("arbitrary",), vmem_limit_bytes=V7X_VMEM_LIMIT)
_W_IN_SCRATCH = pltpu.VMEM((D_MODEL, IN_COLS), BF16)
_W_OUT_SCRATCH = pltpu.VMEM((MIX_WIDTH, D_MODEL), BF16)


def _first_projection(x2d, gain, w, rope, seq):
    rows = x2d.shape[0]
    out_shape, out_specs = _proj_out(rows)
    return pl.pallas_call(
        _first_proj_kernel,
        grid=(rows // PROJ_ROWS,),
        in_specs=[_row_spec(D_MODEL), _const_spec((1, D_MODEL)), _layer_spec(0, (D_MODEL, IN_COLS))]
        + _rope_specs(seq),
        out_specs=out_specs,
        out_shape=out_shape,
        scratch_shapes=[_W_IN_SCRATCH],
        compiler_params=_PARAMS_1D,
        name="first_projection",
    )(x2d, gain, w, *rope)


def _mix_specs(layer, seq, mem_len):
    steps_per_batch = seq // PROJ_ROWS
    mem_spec = pl.BlockSpec((1, 1, mem_len, MEM_WIDTH), lambda r: (layer, r // steps_per_batch, 0, 0))
    gate_spec = pl.BlockSpec((PROJ_ROWS, MEM_WIDTH), lambda r: (r, _GATE_OFFSET["mem_g"] // MEM_WIDTH))
    return [_row_spec(D_MODEL), _row_spec(SB_WIDTH), _row_spec(MOBA_WIDTH), _row_spec(MEM_WIDTH),
            mem_spec, mem_spec, gate_spec, _layer_spec(layer, (MIX_WIDTH, D_MODEL))]


def _next_projection(mix, layer, gain, w, rope, seq):
    rows = mix[0].shape[0]
    out_shape, out_specs = _proj_out(rows)
    return pl.pallas_call(
        _next_proj_kernel,
        grid=(rows // PROJ_ROWS,),
        in_specs=_mix_specs(layer, seq, mix[4].shape[2])
        + [_const_spec((1, D_MODEL)), _layer_spec(layer + 1, (D_MODEL, IN_COLS))] + _rope_specs(seq),
        out_specs=[_row_spec(D_MODEL)] + out_specs,
        out_shape=[jax.ShapeDtypeStruct((rows, D_MODEL), F32)] + out_shape,
        scratch_shapes=[_W_IN_SCRATCH, _W_OUT_SCRATCH],
        compiler_params=_PARAMS_1D,
        name="next_projection",
    )(*mix, gain, w, *rope)


def _final_projection(mix, layer, gain, seq):
    rows = mix[0].shape[0]
    return pl.pallas_call(
        _final_kernel,
        grid=(rows // PROJ_ROWS,),
        in_specs=_mix_specs(layer, seq, mix[4].shape[2]) + [_const_spec((1, D_MODEL))],
        out_specs=_row_spec(D_MODEL),
        out_shape=jax.ShapeDtypeStruct((rows, D_MODEL), F32),
        scratch_shapes=[_W_OUT_SCRATCH],
        compiler_params=_PARAMS_1D,
        name="final_projection",
    )(*mix, gain)


def _mem_kv_kernel(mem_ref, gain_ref, w_ref, mk_ref, mv_ref):
    batch, mem_len, _ = mem_ref.shape
    m = _rms_norm(mem_ref[...].reshape(batch * mem_len, D_MODEL), gain_ref[0]).astype(BF16)
    kv = jnp.dot(m, w_ref[0].astype(BF16), preferred_element_type=F32)
    mk_ref[0] =kv[:, :MEM_WIDTH].astype(BF16).reshape(batch, mem_len, MEM_WIDTH)
    mv_ref[0] = kv[:, MEM_WIDTH:].astype(BF16).reshape(batch, mem_len, MEM_WIDTH)


def _mem_kv(mem, gains, w):
    batch, mem_len, _ = mem.shape
    out = jax.ShapeDtypeStruct((DEPTH, batch, mem_len, MEM_WIDTH), BF16)
    out_spec = pl.BlockSpec((1, batch, mem_len, MEM_WIDTH), lambda l: (l, 0, 0, 0))
    return pl.pallas_call(
        _mem_kv_kernel,
        grid=(DEPTH,),
        in_specs=[pl.BlockSpec((batch, mem_len, D_MODEL), lambda l: (0, 0, 0)),
                  pl.BlockSpec((1, 1, D_MODEL), lambda l: (l, 0, 0)),
                  pl.BlockSpec((1, D_MODEL, 2 * MEM_WIDTH), lambda l: (l, 0, 0))],
        out_specs=[out_spec, out_spec],
        out_shape=[out, out],
        compiler_params=_PARAMS_1D,
        name="memory_kv",
    )(mem, gains, w)


def _stack_heads(q_pair):
    lane = lax.broadcasted_iota(jnp.int32, q_pair.shape, 1)
    zero = jnp.zeros_like(q_pair)
    return jnp.concatenate([jnp.where(lane < HEAD_DIM, q_pair, zero), jnp.where(lane >= HEAD_DIM, q_pair, zero)],
                           axis=0)


def _unstack_heads(stacked):
    rows = stacked.shape[0] // 2
    lane = lax.broadcasted_iota(jnp.int32, (rows, stacked.shape[1]), 1)
    return jnp.where(lane < HEAD_DIM, stacked[:rows], stacked[rows:])


def _silu(g):
    return g * jax.nn.sigmoid(g)


def _local_positions(keys):
    row = lax.broadcasted_iota(jnp.int32, (2 * Q_ROWS, keys), 0) % Q_ROWS
    col = lax.broadcasted_iota(jnp.int32, (2 * Q_ROWS, keys), 1)
    return row, col


def _softplus2(z2):
    return jnp.maximum(z2, 0.0) + jnp.log2(1.0 + jnp.exp2(-jnp.abs(z2)))


def _sb_prepare(k_ref, v_ref, krev_ref, vrev_ref):
    seq = v_ref.shape[1]
    for c in range(seq // CUM_BLOCK):
        src = slice(c * CUM_BLOCK, (c + 1) * CUM_BLOCK)
        dst = slice(seq - (c + 1) * CUM_BLOCK, seq - c * CUM_BLOCK)
        krev_ref[dst, :] = k_ref[0, src, :]
        vrev_ref[dst, :] = v_ref[0, src, :]


def _sb_block_pair(m, q_ref, g_ref, cum_ref, o_ref, krev_ref, vrev_ref):
    row, col = _local_positions(CUM_BLOCK)
    past = col < row
    seq = krev_ref.shape[0]
    half = 2 * Q_ROWS
    first, second = 2 * m, 2 * m + 1
    block_rows = [slice(n * Q_ROWS, (n + 1) * Q_ROWS) for n in (first, second)]
    q = [_stack_heads(q_ref[0, r, :]) for r in block_rows]
    shared = slice(seq - (first + 1) * KEY_BLOCK, seq)
    own = slice(seq - (second + 1) * KEY_BLOCK, seq - second * KEY_BLOCK)
    chunks = (first + 1) * KEY_BLOCK // CUM_BLOCK
    z_own = lax.dot_general(q[1], krev_ref[own, :], _NT, preferred_element_type=F32)
    z = lax.dot_general(jnp.concatenate(q, axis=0), krev_ref[shared, :], _NT, preferred_element_type=F32)

    def masked(x, c, block):
        return jnp.where(past, x, 0.0) if (c == 0 and block == 0) else x

    soft_own = jnp.where(past, _softplus2(z_own), 0.0)
    pieces = [soft_own.astype(BF16)]
    tails = [None, jnp.sum(soft_own, axis=-1, keepdims=True)]
    totals = []
    for c in range(chunks):
        soft = [masked(_softplus2(z[b * half:(b + 1) * half, c * CUM_BLOCK:(c + 1) * CUM_BLOCK]), c, b)
                for b in range(2)]
        pieces += [s.astype(BF16) for s in soft]
        totals.append([jnp.sum(s, axis=-1, keepdims=True) for s in soft])
    sums = jnp.dot(jnp.concatenate(pieces, axis=0), cum_ref[...], preferred_element_type=F32)
    w_own = jnp.where(past, jnp.exp2(z_own + sums[:half]), 0.0).astype(BF16)
    weights = []
    for c in range(chunks):
        w = []
        for b in range(2):
            e = (z[b * half:(b + 1) * half, c * CUM_BLOCK:(c + 1) * CUM_BLOCK]
                 + sums[half * (1 + 2 * c + b):half * (2 + 2 * c + b)])
            if tails[b] is not None:
                e = e - tails[b]
            w.append(masked(jnp.exp2(e), c, b).astype(BF16))
            tails[b] = totals[c][b] if tails[b] is None else tails[b] + totals[c][b]
        weights.append(jnp.concatenate(w, axis=0))
    acc = jnp.dot(jnp.concatenate(weights, axis=1), vrev_ref[shared, :], preferred_element_type=F32)
    acc_own = jnp.dot(w_own, vrev_ref[own, :], preferred_element_type=F32)
    outs = [acc[:half], acc[half:] + acc_own]
    for r, out in zip(block_rows, outs):
        o_ref[0, r, :] = (_unstack_heads(out) * _silu(g_ref[0, r, :])).astype(BF16)


def _softmax_pv(blocks, v_ones):
    m = functools.reduce(jnp.maximum, blocks)
    m = jnp.max(jnp.maximum(m[:, :LANES], m[:, LANES:]), axis=-1, keepdims=True)
    p = jnp.concatenate([jnp.exp2(b - m).astype(BF16) for b in blocks], axis=1)
    acc = jnp.dot(p, v_ones, preferred_element_type=F32)
    return acc[:, :PAIR] / acc[:, PAIR:]


def _unselected(q, kmean, n):
    blocks = kmean.shape[0]
    km_hi = kmean.astype(BF16).astype(F32)
    pieces = jnp.concatenate([km_hi, kmean - km_hi], axis=0).astype(BF16)
    g2 = lax.dot_general(pieces, q, _NT, preferred_element_type=F32)
    gate = g2[:blocks] + g2[blocks:]
    blk = lax.broadcasted_iota(jnp.int32, gate.shape, 0)
    gate = jnp.where(blk < n, gate, -jnp.inf)
    ahead = jnp.zeros_like(gate)
    for j in range(n):
        gj = gate[j:j + 1, :]
        ahead = ahead + jnp.where((gj > gate) | ((gj == gate) & (blk > j)), 1.0, 0.0)
    unsel = jnp.where((ahead >= MOBA_TOPK) & (blk < n), 1.0, 0.0)
    unsel = jnp.concatenate([unsel, jnp.zeros((LANES - blocks, unsel.shape[1]), F32)], axis=0)
    return unsel.T.astype(BF16)


def _moba_prepare(k_ref, v_ref, ind_ref, kaug_ref, vaug_ref):
    kaug_ref[:, :PAIR] = k_ref[0]
    kaug_ref[:, PAIR:] = ind_ref[...]
    vaug_ref[:, :PAIR] = v_ref[0]
    vaug_ref[:, PAIR:] = jnp.ones(v_ref.shape[1:], BF16)


def _row_max(tiles):
    m = functools.reduce(jnp.maximum, tiles)
    return jnp.max(jnp.maximum(m[:, :LANES], m[:, LANES:]), axis=-1, keepdims=True)


def _moba_block_pair(m, q_ref, k_ref, kmean_ref, g_ref, o_ref, kaug_ref, vaug_ref):
    row, col = _local_positions(KEY_BLOCK)
    causal = col <= row
    half = 2 * Q_ROWS
    first, second = 2 * m, 2 * m + 1
    block_rows = [slice(n * Q_ROWS, (n + 1) * Q_ROWS) for n in (first, second)]
    assert (MOBA_TOPK + 1) % 2 == 0, "a block pair must not straddle the all-selected threshold"
    gated = first > MOBA_TOPK
    lhs = []
    for n, r in zip((first, second), block_rows):
        q = _stack_heads(q_ref[0, r, :])
        lhs.append(jnp.concatenate([q, _unselected(q, kmean_ref[0], n)], axis=1) if gated else q)

    def keys(lo, hi):
        return kaug_ref[lo:hi, :] if gated else k_ref[0, lo:hi, :]

    shared = (first + 1) * KEY_BLOCK
    own = slice(second * KEY_BLOCK, (second + 1) * KEY_BLOCK)
    s = lax.dot_general(jnp.concatenate(lhs, axis=0), keys(0, shared), _NT, preferred_element_type=F32)
    s_own = jnp.where(causal, lax.dot_general(lhs[1], keys(own.start, own.stop), _NT,
                                              preferred_element_type=F32), NEG_INF)
    tiles = [[s[b * half:(b + 1) * half, j * KEY_BLOCK:(j + 1) * KEY_BLOCK] for j in range(first + 1)]
             for b in range(2)]
    tiles[0][first] = jnp.where(causal, tiles[0][first], NEG_INF)
    maxes = [_row_max(tiles[0]), _row_max(tiles[1] + [s_own])]
    p = jnp.concatenate([jnp.concatenate([jnp.exp2(tiles[b][j] - maxes[b]).astype(BF16) for b in range(2)], axis=0)
                         for j in range(first + 1)], axis=1)
    acc = jnp.dot(p, vaug_ref[:shared, :], preferred_element_type=F32)
    acc_own = jnp.dot(jnp.exp2(s_own - maxes[1]).astype(BF16), vaug_ref[own, :], preferred_element_type=F32)
    for r, a in zip(block_rows, (acc[:half], acc[half:] + acc_own)):
        out = _unstack_heads(a[:, :PAIR] / a[:, PAIR:])
        o_ref[0, r, :] = (out * _silu(g_ref[0, r, :])).astype(BF16)


def _sb_moba_kernel(sq_ref, sk_ref, sv_ref, sg_ref, cum_ref, mq_ref, mk_ref, mv_ref, kmean_ref, ind_ref, mg_ref,
                    so_ref, mo_ref, krev_ref, vrev_ref, kaug_ref, vaug_ref):
    _sb_prepare(sk_ref, sv_ref, krev_ref, vrev_ref)
    _moba_prepare(mk_ref, mv_ref, ind_ref, kaug_ref, vaug_ref)
    n_blocks = sk_ref.shape[1] // KEY_BLOCK
    for m in reversed(range(n_blocks // 2)):
        _sb_block_pair(m, sq_ref, sg_ref, cum_ref, so_ref, krev_ref, vrev_ref)
        _moba_block_pair(m, mq_ref, mk_ref, kmean_ref, mg_ref, mo_ref, kaug_ref, vaug_ref)


_PARAMS_3D = pltpu.CompilerParams(dimension_semantics=("arbitrary", "arbitrary", "arbitrary"),
                                  vmem_limit_bytes=V7X_VMEM_LIMIT)


def _seq_spec(seq, col_block_offset=0):
    return pl.BlockSpec((1, seq, PAIR), lambda b, p, i: (b, 0, p + col_block_offset))


def _neg_cum_matrix():
    j = np.arange(CUM_BLOCK)
    return jnp.asarray(-(j[:, None] >= j[None, :]).astype(np.float32), dtype=BF16)


def _block_indicator(seq):
    s = np.arange(seq)[:, None] // MOBA_BLOCK
    return jnp.asarray(np.where(s == np.arange(LANES)[None, :], NEG_INF, 0.0), dtype=BF16)


def _sb_moba_attention(sb_qkv, mb_qkv, kmean, gates):
    batch, seq, width = sb_qkv[0].shape
    assert SB_WIDTH == MOBA_WIDTH == width
    qkv_specs = [_seq_spec(seq)] * 3
    out = jax.ShapeDtypeStruct((batch, seq, width), BF16)
    return pl.pallas_call(
        _sb_moba_kernel,
        grid=(batch, width // PAIR, 1),
        in_specs=qkv_specs + [_seq_spec(seq, _GATE_OFFSET["sb_g"] // PAIR),
                              pl.BlockSpec((CUM_BLOCK, CUM_BLOCK), lambda b, p, i: (0, 0))]
        + qkv_specs + [pl.BlockSpec((1, seq // MOBA_BLOCK, PAIR), lambda b, p, i: (b, 0, p)),
                       pl.BlockSpec((seq, LANES), lambda b, p, i: (0, 0)),
                       _seq_spec(seq, _GATE_OFFSET["mb_g"] // PAIR)],
        out_specs=[_seq_spec(seq), _seq_spec(seq)],
        out_shape=[out, out],
        scratch_shapes=[pltpu.VMEM((seq, PAIR), BF16), pltpu.VMEM((seq, PAIR), BF16),
                        pltpu.VMEM((seq, 2 * PAIR), BF16), pltpu.VMEM((seq, 2 * PAIR), BF16)],
        compiler_params=_PARAMS_3D,
        name="sb_moba_attention",
    )(*sb_qkv, gates, _neg_cum_matrix(), *mb_qkv, kmean, _block_indicator(seq), gates)


def _rope_tables(seq):
    inv_freq = np.float32(ROPE_THETA) ** (-np.arange(ROPE_HALF, dtype=np.float32) * np.float32(2.0 / ROPE_DIMS))
    ang = np.arange(seq, dtype=np.float32)[:, None] * inv_freq[None, :].astype(np.float32)
    cos, sin = np.cos(ang).astype(np.float32), np.sin(ang).astype(np.float32)
    d = np.arange(LANES) % HEAD_DIM
    f = d % ROPE_HALF
    cos_t = np.where(d < ROPE_DIMS, cos[:, f], np.float32(1.0))
    sin_up = np.where((d >= ROPE_HALF) & (d < ROPE_DIMS), sin[:, f], np.float32(0.0))
    sin_dn = np.where(d < ROPE_HALF, -sin[:, f], np.float32(0.0))
    return tuple(jnp.asarray(t, dtype=F32) for t in (cos_t, sin_up, sin_dn))


def kernel(x, mem, norm_g, w_in, mem_norm_g, w_mem_kv, w_out, final_norm_g):
    batch, seq, d_model = x.shape
    assert d_model == D_MODEL and seq % PROJ_ROWS == 0 and seq % (2 * KEY_BLOCK) == 0 and Q_ROWS == KEY_BLOCK
    assert seq // MOBA_BLOCK <= LANES
    rows = batch * seq
    rope = _rope_tables(seq)
    mk, mv = _mem_kv(mem, mem_norm_g.reshape(DEPTH, 1, D_MODEL), w_mem_kv)

    x2d = x.reshape(rows, D_MODEL)
    proj = _first_projection(x2d, norm_g[0:1], w_in, rope, seq)
    for layer in range(DEPTH):
        p = dict(zip(_PROJ_OUT_NAMES, proj))
        seq3 = lambda a: a.reshape(batch, seq, a.shape[-1])
        gates = seq3(p["gates"])
        kmean = p["kmean"].reshape(batch, seq // MOBA_BLOCK, MOBA_WIDTH)
        o_sb, o_mb = _sb_moba_attention([seq3(p[name]) for name in ("sb_q", "sb_k", "sb_v")],
                                        [seq3(p[name]) for name in ("mb_q", "mb_k", "mb_v")], kmean, gates)
        mix = [x2d, o_sb.reshape(rows, SB_WIDTH), o_mb.reshape(rows, MOBA_WIDTH), p["mem_q"], mk, mv, p["gates"],
               w_out]
        if layer + 1 < DEPTH:
            x2d, *proj = _next_projection(mix, layer, norm_g[layer + 1:layer + 2], w_in, rope, seq)
        else:
            out = _final_projection(mix, layer, final_norm_g.reshape(1, D_MODEL), seq)
    return out.reshape(batch, seq, D_MODEL)
```

```python
import functools

import numpy as np
import jax
import jax.numpy as jnp
from jax import lax
from jax.experimental import pallas as pl
from jax.experimental.pallas import tpu as pltpu

D_MODEL = 1024
DEPTH = 2
HEAD_DIM = 64
MEM_HEADS = 4
SB_HEADS = 6
MOBA_HEADS = 6
MEM_WIDTH = MEM_HEADS * HEAD_DIM
SB_WIDTH = SB_HEADS * HEAD_DIM
MOBA_WIDTH = MOBA_HEADS * HEAD_DIM
MIX_WIDTH = SB_WIDTH + MOBA_WIDTH + MEM_WIDTH
IN_COLS = 4 * SB_WIDTH + 4 * MOBA_WIDTH + 2 * MEM_WIDTH
MOBA_BLOCK = 256
MOBA_TOPK = 3
ROPE_THETA = 500000.0
ROPE_DIMS = HEAD_DIM // 4
ROPE_HALF = ROPE_DIMS // 2
NORM_EPS = 1e-6
NEG_INF = -1e30
QK_SCALE = HEAD_DIM ** -0.5
LOG2_E = 1.4426950408889634

LANES = 128
V7X_VMEM_LIMIT = 56 * 1024 * 1024

PAIR = LANES
PROJ_ROWS = 512
PROJ_CHUNK = 512
Q_ROWS = 256
KEY_BLOCK = 256
CUM_BLOCK = 256

F32 = jnp.float32
BF16 = jnp.bfloat16
_NT = (((1,), (1,)), ((), ()))

_GROUPS = (("sb_q", SB_WIDTH), ("sb_k", SB_WIDTH), ("sb_v", SB_WIDTH), ("sb_g", SB_WIDTH),
           ("mb_q", MOBA_WIDTH), ("mb_k", MOBA_WIDTH), ("mb_v", MOBA_WIDTH), ("mb_g", MOBA_WIDTH),
           ("mem_q", MEM_WIDTH), ("mem_g", MEM_WIDTH))
_GATE_OFFSET = {"sb_g": 0, "mb_g": SB_WIDTH, "mem_g": SB_WIDTH + MOBA_WIDTH}


def _unit_table():
    units = []
    for name, width in _GROUPS:
        for off in range(0, width, LANES):
            units.append((name, off))
    return units


_UNITS = _unit_table()


def _rms_norm(x, gain):
    ms = jnp.mean(x * x, axis=-1, keepdims=True)
    return x * lax.rsqrt(ms + NORM_EPS) * gain


def _rotary(y, cos, sin_up, sin_dn):
    return y * cos + pltpu.roll(y, ROPE_HALF, 1) * sin_up + pltpu.roll(y, LANES - ROPE_HALF, 1) * sin_dn


def _project(x, gain_ref, w_ref, cos_ref, sup_ref, sdn_ref, outs):
    h = _rms_norm(x, gain_ref[...]).astype(BF16)
    cos, sup, sdn = cos_ref[...], sup_ref[...], sdn_ref[...]
    units_per_chunk = PROJ_CHUNK // LANES
    for c in range(IN_COLS // PROJ_CHUNK):
        y = jnp.dot(h, w_ref[:, c * PROJ_CHUNK:(c + 1) * PROJ_CHUNK], preferred_element_type=F32)
        for u in range(units_per_chunk):
            name, off = _UNITS[c * units_per_chunk + u]
            piece = y[:, u * LANES:(u + 1) * LANES]
            if name in _GATE_OFFSET:
                g0 = _GATE_OFFSET[name] + off
                outs["gates"][:, g0:g0 + LANES] = piece
                continue
            if name in ("mb_q", "mb_k"):
                piece = _rotary(piece, cos, sup, sdn)
            if name == "mb_k":
                for blk in range(PROJ_ROWS // MOBA_BLOCK):
                    rows = piece[blk * MOBA_BLOCK:(blk + 1) * MOBA_BLOCK]
                    outs["kmean"][0, blk:blk + 1, off:off + LANES] = jnp.mean(rows, axis=0, keepdims=True)
            if name.endswith("_q"):
                piece = piece * (QK_SCALE * LOG2_E)
            outs[name][:, off:off + LANES] = piece.astype(BF16)


_PROJ_OUT_NAMES = ("sb_q", "sb_k", "sb_v", "mb_q", "mb_k", "mb_v", "mem_q", "gates", "kmean")


def _cast_weight_once(w32_ref, w_ref):
    @pl.when(pl.program_id(0) == 0)
    def _():
        cols = w_ref.shape[1]
        chunk = min(cols, PROJ_CHUNK)
        for c in range(cols // chunk):
            w_ref[:, c * chunk:(c + 1) * chunk] = w32_ref[0, :, c * chunk:(c + 1) * chunk].astype(BF16)


def _first_proj_kernel(x_ref, gain_ref, w32_ref, cos_ref, sup_ref, sdn_ref, *rest):
    out_refs, (w_ref,) = rest[:len(_PROJ_OUT_NAMES)], rest[len(_PROJ_OUT_NAMES):]
    _cast_weight_once(w32_ref, w_ref)
    outs = dict(zip(_PROJ_OUT_NAMES, out_refs))
    _project(x_ref[...], gain_ref, w_ref, cos_ref, sup_ref, sdn_ref, outs)


def _memory_attention(q_ref, mk_ref, mv_ref, g_ref):
    outs = []
    for p in range(MEM_WIDTH // PAIR):
        cols = slice(p * PAIR, (p + 1) * PAIR)
        mv = mv_ref[0, 0, :, cols]
        v_ones = jnp.concatenate([mv, jnp.ones_like(mv)], axis=1)
        s = lax.dot_general(_stack_heads(q_ref[:, cols]), mk_ref[0, 0, :, cols], _NT, preferred_element_type=F32)
        outs.append((_unstack_heads(_softmax_pv([s], v_ones)) * _silu(g_ref[:, cols])).astype(BF16))
    return jnp.concatenate(outs, axis=1)


def _residual(x_ref, osb_ref, omb_ref, memq_ref, mk_ref, mv_ref, memg_ref, wo_ref):
    o_mem = _memory_attention(memq_ref, mk_ref, mv_ref, memg_ref)
    mixed = jnp.concatenate([osb_ref[...], omb_ref[...], o_mem], axis=1)
    return x_ref[...] + jnp.dot(mixed, wo_ref[...], preferred_element_type=F32)


_MIX_REFS = 8


def _next_proj_kernel(*refs):
    mix, (gain_ref, w32_ref, cos_ref, sup_ref, sdn_ref, xnew_ref) = refs[:_MIX_REFS], refs[_MIX_REFS:_MIX_REFS + 6]
    out_refs, (w_ref, wo_ref) = refs[_MIX_REFS + 6:-2], refs[-2:]
    _cast_weight_once(w32_ref, w_ref)
    _cast_weight_once(mix[-1], wo_ref)
    outs = dict(zip(_PROJ_OUT_NAMES, out_refs))
    xnew = _residual(*mix[:-1], wo_ref)
    xnew_ref[...] = xnew
    _project(xnew, gain_ref, w_ref, cos_ref, sup_ref, sdn_ref, outs)


def _final_kernel(*refs):
    mix, (gain_ref, out_ref, wo_ref) = refs[:_MIX_REFS], refs[_MIX_REFS:]
    _cast_weight_once(mix[-1], wo_ref)
    out_ref[...] = _rms_norm(_residual(*mix[:-1], wo_ref), gain_ref[...])


def _row_spec(cols):
    return pl.BlockSpec((PROJ_ROWS, cols), lambda r: (r, 0))


def _const_spec(shape):
    return pl.BlockSpec(shape, lambda r: (0,) * len(shape))


def _layer_spec(layer, shape):
    return pl.BlockSpec((1,) + shape, lambda r: (layer,) + (0,) * len(shape))


def _proj_out(rows):
    shapes = {"sb_q": SB_WIDTH, "sb_k": SB_WIDTH, "sb_v": SB_WIDTH, "mb_q": MOBA_WIDTH, "mb_k": MOBA_WIDTH,
              "mb_v": MOBA_WIDTH, "mem_q": MEM_WIDTH}
    out_shape, out_specs = [], []
    for name in _PROJ_OUT_NAMES:
        if name == "gates":
            out_shape.append(jax.ShapeDtypeStruct((rows, MIX_WIDTH), F32))
            out_specs.append(_row_spec(MIX_WIDTH))
        elif name == "kmean":
            blocks = PROJ_ROWS // MOBA_BLOCK
            out_shape.append(jax.ShapeDtypeStruct((rows // PROJ_ROWS, blocks, MOBA_WIDTH), F32))
            out_specs.append(pl.BlockSpec((1, blocks, MOBA_WIDTH), lambda r: (r, 0, 0)))
        else:
            out_shape.append(jax.ShapeDtypeStruct((rows, shapes[name]), BF16))
            out_specs.append(_row_spec(shapes[name]))
    return out_shape, out_specs


def _rope_specs(seq):
    steps = seq // PROJ_ROWS
    return [pl.BlockSpec((PROJ_ROWS, LANES), lambda r: (r % steps, 0))] * 3


_PARAMS_1D = pltpu.CompilerParams(dimension_semantics=("arbitrary",), vmem_limit_bytes=V7X_VMEM_LIMIT)
_W_IN_SCRATCH = pltpu.VMEM((D_MODEL, IN_COLS), BF16)
_W_OUT_SCRATCH = pltpu.VMEM((MIX_WIDTH, D_MODEL), BF16)


def _first_projection(x2d, gain, w, rope, seq):
    rows = x2d.shape[0]
    out_shape, out_specs = _proj_out(rows)
    return pl.pallas_call(
        _first_proj_kernel,
        grid=(rows // PROJ_ROWS,),
        in_specs=[_row_spec(D_MODEL), _const_spec((1, D_MODEL)), _layer_spec(0, (D_MODEL, IN_COLS))]
        + _rope_specs(seq),
        out_specs=out_specs,
        out_shape=out_shape,
        scratch_shapes=[_W_IN_SCRATCH],
        compiler_params=_PARAMS_1D,
        name="first_projection",
    )(x2d, gain, w, *rope)


def _mix_specs(layer, seq, mem_len):
    steps_per_batch = seq // PROJ_ROWS
    mem_spec = pl.BlockSpec((1, 1, mem_len, MEM_WIDTH), lambda r: (layer, r // steps_per_batch, 0, 0))
    gate_spec = pl.BlockSpec((PROJ_ROWS, MEM_WIDTH), lambda r: (r, _GATE_OFFSET["mem_g"] // MEM_WIDTH))
    return [_row_spec(D_MODEL), _row_spec(SB_WIDTH), _row_spec(MOBA_WIDTH), _row_spec(MEM_WIDTH),
            mem_spec, mem_spec, gate_spec, _layer_spec(layer, (MIX_WIDTH, D_MODEL))]


def _next_projection(mix, layer, gain, w, rope, seq):
    rows = mix[0].shape[0]
    out_shape, out_specs = _proj_out(rows)
    return pl.pallas_call(
        _next_proj_kernel,
        grid=(rows // PROJ_ROWS,),
        in_specs=_mix_specs(layer, seq, mix[4].shape[2])
        + [_const_spec((1, D_MODEL)), _layer_spec(layer + 1, (D_MODEL, IN_COLS))] + _rope_specs(seq),
        out_specs=[_row_spec(D_MODEL)] + out_specs,
        out_shape=[jax.ShapeDtypeStruct((rows, D_MODEL), F32)] + out_shape,
        scratch_shapes=[_W_IN_SCRATCH, _W_OUT_SCRATCH],
        compiler_params=_PARAMS_1D,
        name="next_projection",
    )(*mix, gain, w, *rope)


def _final_projection(mix, layer, gain, seq):
    rows = mix[0].shape[0]
    return pl.pallas_call(
        _final_kernel,
        grid=(rows // PROJ_ROWS,),
        in_specs=_mix_specs(layer, seq, mix[4].shape[2]) + [_const_spec((1, D_MODEL))],
        out_specs=_row_spec(D_MODEL),
        out_shape=jax.ShapeDtypeStruct((rows, D_MODEL), F32),
        scratch_shapes=[_W_OUT_SCRATCH],
        compiler_params=_PARAMS_1D,
        name="final_projection",
    )(*mix, gain)


def _mem_kv_kernel(mem_ref, gain_ref, w_ref, mk_ref, mv_ref):
    batch, mem_len, _ = mem_ref.shape
    m = _rms_norm(mem_ref[...].reshape(batch * mem_len, D_MODEL), gain_ref[0]).astype(BF16)
    kv = jnp.dot(m, w_ref[0].astype(BF16), preferred_element_type=F32)
    mk_ref[0] =kv[:, :MEM_WIDTH].astype(BF16).reshape(batch, mem_len, MEM_WIDTH)
    mv_ref[0] = kv[:, MEM_WIDTH:].astype(BF16).reshape(batch, mem_len, MEM_WIDTH)


def _mem_kv(mem, gains, w):
    batch, mem_len, _ = mem.shape
    out = jax.ShapeDtypeStruct((DEPTH, batch, mem_len, MEM_WIDTH), BF16)
    out_spec = pl.BlockSpec((1, batch, mem_len, MEM_WIDTH), lambda l: (l, 0, 0, 0))
    return pl.pallas_call(
        _mem_kv_kernel,
        grid=(DEPTH,),
        in_specs=[pl.BlockSpec((batch, mem_len, D_MODEL), lambda l: (0, 0, 0)),
                  pl.BlockSpec((1, 1, D_MODEL), lambda l: (l, 0, 0)),
                  pl.BlockSpec((1, D_MODEL, 2 * MEM_WIDTH), lambda l: (l, 0, 0))],
        out_specs=[out_spec, out_spec],
        out_shape=[out, out],
        compiler_params=_PARAMS_1D,
        name="memory_kv",
    )(mem, gains, w)


def _stack_heads(q_pair):
    lane = lax.broadcasted_iota(jnp.int32, q_pair.shape, 1)
    zero = jnp.zeros_like(q_pair)
    return jnp.concatenate([jnp.where(lane < HEAD_DIM, q_pair, zero), jnp.where(lane >= HEAD_DIM, q_pair, zero)],
                           axis=0)


def _unstack_heads(stacked):
    rows = stacked.shape[0] // 2
    lane = lax.broadcasted_iota(jnp.int32, (rows, stacked.shape[1]), 1)
    return jnp.where(lane < HEAD_DIM, stacked[:rows], stacked[rows:])


def _silu(g):
    return g * jax.nn.sigmoid(g)


def _local_positions(keys):
    row = lax.broadcasted_iota(jnp.int32, (2 * Q_ROWS, keys), 0) % Q_ROWS
    col = lax.broadcasted_iota(jnp.int32, (2 * Q_ROWS, keys), 1)
    return row, col


def _softplus2(z2):
    return jnp.maximum(z2, 0.0) + jnp.log2(1.0 + jnp.exp2(-jnp.abs(z2)))


def _sb_prepare(k_ref, v_ref, krev_ref, vrev_ref):
    seq = v_ref.shape[1]
    for c in range(seq // CUM_BLOCK):
        src = slice(c * CUM_BLOCK, (c + 1) * CUM_BLOCK)
        dst = slice(seq - (c + 1) * CUM_BLOCK, seq - c * CUM_BLOCK)
        krev_ref[dst, :] = k_ref[0, src, :]
        vrev_ref[dst, :] = v_ref[0, src, :]


def _sb_block_pair(m, q_ref, g_ref, cum_ref, o_ref, krev_ref, vrev_ref):
    row, col = _local_positions(CUM_BLOCK)
    past = col < row
    seq = krev_ref.shape[0]
    half = 2 * Q_ROWS
    first, second = 2 * m, 2 * m + 1
    block_rows = [slice(n * Q_ROWS, (n + 1) * Q_ROWS) for n in (first, second)]
    q = [_stack_heads(q_ref[0, r, :]) for r in block_rows]
    shared = slice(seq - (first + 1) * KEY_BLOCK, seq)
    own = slice(seq - (second + 1) * KEY_BLOCK, seq - second * KEY_BLOCK)
    chunks = (first + 1) * KEY_BLOCK // CUM_BLOCK
    z_own = lax.dot_general(q[1], krev_ref[own, :], _NT, preferred_element_type=F32)
    z = lax.dot_general(jnp.concatenate(q, axis=0), krev_ref[shared, :], _NT, preferred_element_type=F32)

    def masked(x, c, block):
        return jnp.where(past, x, 0.0) if (c == 0 and block == 0) else x

    soft_own = jnp.where(past, _softplus2(z_own), 0.0)
    pieces = [soft_own.astype(BF16)]
    tails = [None, jnp.sum(soft_own, axis=-1, keepdims=True)]
    totals = []
    for c in range(chunks):
        soft = [masked(_softplus2(z[b * half:(b + 1) * half, c * CUM_BLOCK:(c + 1) * CUM_BLOCK]), c, b)
                for b in range(2)]
        pieces += [s.astype(BF16) for s in soft]
        totals.append([jnp.sum(s, axis=-1, keepdims=True) for s in soft])
    sums = jnp.dot(jnp.concatenate(pieces, axis=0), cum_ref[...], preferred_element_type=F32)
    w_own = jnp.where(past, jnp.exp2(z_own + sums[:half]), 0.0).astype(BF16)
    weights = []
    for c in range(chunks):
        w = []
        for b in range(2):
            e = (z[b * half:(b + 1) * half, c * CUM_BLOCK:(c + 1) * CUM_BLOCK]
                 + sums[half * (1 + 2 * c + b):half * (2 + 2 * c + b)])
            if tails[b] is not None:
                e = e - tails[b]
            w.append(masked(jnp.exp2(e), c, b).astype(BF16))
            tails[b] = totals[c][b] if tails[b] is None else tails[b] + totals[c][b]
        weights.append(jnp.concatenate(w, axis=0))
    acc = jnp.dot(jnp.concatenate(weights, axis=1), vrev_ref[shared, :], preferred_element_type=F32)
    acc_own = jnp.dot(w_own, vrev_ref[own, :], preferred_element_type=F32)
    outs = [acc[:half], acc[half:] + acc_own]
    for r, out in zip(block_rows, outs):
        o_ref[0, r, :] = (_unstack_heads(out) * _silu(g_ref[0, r, :])).astype(BF16)


def _softmax_pv(blocks, v_ones):
    m = _row_max(blocks)
    p = jnp.concatenate([jnp.exp2(b - m).astype(BF16) for b in blocks], axis=1)
    acc = jnp.dot(p, v_ones, preferred_element_type=F32)
    return acc[:, :PAIR] / acc[:, PAIR:]


def _unselected(q, kmean, n):
    blocks = kmean.shape[0]
    km_hi = kmean.astype(BF16).astype(F32)
    pieces = jnp.concatenate([km_hi, kmean - km_hi], axis=0).astype(BF16)
    g2 = lax.dot_general(pieces, q, _NT, preferred_element_type=F32)
    gate = g2[:blocks] + g2[blocks:]
    blk = lax.broadcasted_iota(jnp.int32, gate.shape, 0)
    gate = jnp.where(blk < n, gate, -jnp.inf)
    ahead = jnp.zeros_like(gate)
    for j in range(n):
        gj = gate[j:j + 1, :]
        ahead = ahead + jnp.where((gj > gate) | ((gj == gate) & (blk > j)), 1.0, 0.0)
    unsel = jnp.where((ahead >= MOBA_TOPK) & (blk < n), 1.0, 0.0)
    unsel = jnp.concatenate([unsel, jnp.zeros((LANES - blocks, unsel.shape[1]), F32)], axis=0)
    return unsel.T.astype(BF16)


def _moba_prepare(k_ref, v_ref, ind_ref, kaug_ref, vaug_ref):
    kaug_ref[:, :PAIR] = k_ref[0]
    kaug_ref[:, PAIR:] = ind_ref[...]
    vaug_ref[:, :PAIR] = v_ref[0]
    vaug_ref[:, PAIR:] = jnp.ones(v_ref.shape[1:], BF16)


def _row_max(tiles):
    m = functools.reduce(jnp.maximum, tiles)
    return jnp.max(jnp.maximum(m[:, :LANES], m[:, LANES:]), axis=-1, keepdims=True)


def _moba_block_pair(m, q_ref, k_ref, kmean_ref, g_ref, o_ref, kaug_ref, vaug_ref):
    row, col = _local_positions(KEY_BLOCK)
    causal = col <= row
    half = 2 * Q_ROWS
    first, second = 2 * m, 2 * m + 1
    block_rows = [slice(n * Q_ROWS, (n + 1) * Q_ROWS) for n in (first, second)]
    assert (MOBA_TOPK + 1) % 2 == 0, "a block pair must not straddle the all-selected threshold"
    gated = first > MOBA_TOPK
    lhs = []
    for n, r in zip((first, second), block_rows):
        q = _stack_heads(q_ref[0, r, :])
        lhs.append(jnp.concatenate([q, _unselected(q, kmean_ref[0], n)], axis=1) if gated else q)

    def keys(lo, hi):
        return kaug_ref[lo:hi, :] if gated else k_ref[0, lo:hi, :]

    shared = (first + 1) * KEY_BLOCK
    own = slice(second * KEY_BLOCK, (second + 1) * KEY_BLOCK)
    s = lax.dot_general(jnp.concatenate(lhs, axis=0), keys(0, shared), _NT, preferred_element_type=F32)
    s_own = jnp.where(causal, lax.dot_general(lhs[1], keys(own.start, own.stop), _NT,
                                              preferred_element_type=F32), NEG_INF)
    tiles = [[s[b * half:(b + 1) * half, j * KEY_BLOCK:(j + 1) * KEY_BLOCK] for j in range(first + 1)]
             for b in range(2)]
    tiles[0][first] = jnp.where(causal, tiles[0][first], NEG_INF)
    maxes = [_row_max(tiles[0]), _row_max(tiles[1] + [s_own])]
    p = jnp.concatenate([jnp.concatenate([jnp.exp2(tiles[b][j] - maxes[b]).astype(BF16) for b in range(2)], axis=0)
                         for j in range(first + 1)], axis=1)
    acc = jnp.dot(p, vaug_ref[:shared, :], preferred_element_type=F32)
    acc_own = jnp.dot(jnp.exp2(s_own - maxes[1]).astype(BF16), vaug_ref[own, :], preferred_element_type=F32)
    for r, a in zip(block_rows, (acc[:half], acc[half:] + acc_own)):
        out = _unstack_heads(a[:, :PAIR] / a[:, PAIR:])
        o_ref[0, r, :] = (out * _silu(g_ref[0, r, :])).astype(BF16)


def _sb_moba_kernel(sq_ref, sk_ref, sv_ref, sg_ref, cum_ref, mq_ref, mk_ref, mv_ref, kmean_ref, ind_ref, mg_ref,
                    so_ref, mo_ref, krev_ref, vrev_ref, kaug_ref, vaug_ref):
    _sb_prepare(sk_ref, sv_ref, krev_ref, vrev_ref)
    _moba_prepare(mk_ref, mv_ref, ind_ref, kaug_ref, vaug_ref)
    n_blocks = sk_ref.shape[1] // KEY_BLOCK
    for m in reversed(range(n_blocks // 2)):
        _sb_block_pair(m, sq_ref, sg_ref, cum_ref, so_ref, krev_ref, vrev_ref)
        _moba_block_pair(m, mq_ref, mk_ref, kmean_ref, mg_ref, mo_ref, kaug_ref, vaug_ref)


_PARAMS_3D = pltpu.CompilerParams(dimension_semantics=("arbitrary", "arbitrary", "arbitrary"),
                                  vmem_limit_bytes=V7X_VMEM_LIMIT)


def _seq_spec(seq, col_block_offset=0):
    return pl.BlockSpec((1, seq, PAIR), lambda b, p, i: (b, 0, p + col_block_offset))


def _neg_cum_matrix():
    j = np.arange(CUM_BLOCK)
    return jnp.asarray(-(j[:, None] >= j[None, :]).astype(np.float32), dtype=BF16)


def _block_indicator(seq):
    s = np.arange(seq)[:, None] // MOBA_BLOCK
    return jnp.asarray(np.where(s == np.arange(LANES)[None, :], NEG_INF, 0.0), dtype=BF16)


def _sb_moba_attention(sb_qkv, mb_qkv, kmean, gates):
    batch, seq, width = sb_qkv[0].shape
    assert SB_WIDTH == MOBA_WIDTH == width
    qkv_specs = [_seq_spec(seq)] * 3
    out = jax.ShapeDtypeStruct((batch, seq, width), BF16)
    return pl.pallas_call(
        _sb_moba_kernel,
        grid=(batch, width // PAIR, 1),
        in_specs=qkv_specs + [_seq_spec(seq, _GATE_OFFSET["sb_g"] // PAIR),
                              pl.BlockSpec((CUM_BLOCK, CUM_BLOCK), lambda b, p, i: (0, 0))]
        + qkv_specs + [pl.BlockSpec((1, seq // MOBA_BLOCK, PAIR), lambda b, p, i: (b, 0, p)),
                       pl.BlockSpec((seq, LANES), lambda b, p, i: (0, 0)),
                       _seq_spec(seq, _GATE_OFFSET["mb_g"] // PAIR)],
        out_specs=[_seq_spec(seq), _seq_spec(seq)],
        out_shape=[out, out],
        scratch_shapes=[pltpu.VMEM((seq, PAIR), BF16), pltpu.VMEM((seq, PAIR), BF16),
                        pltpu.VMEM((seq, 2 * PAIR), BF16), pltpu.VMEM((seq, 2 * PAIR), BF16)],
        compiler_params=_PARAMS_3D,
        name="sb_moba_attention",
    )(*sb_qkv, gates, _neg_cum_matrix(), *mb_qkv, kmean, _block_indicator(seq), gates)


def _rope_tables(seq):
    inv_freq = np.float32(ROPE_THETA) ** (-np.arange(ROPE_HALF, dtype=np.float32) * np.float32(2.0 / ROPE_DIMS))
    ang = np.arange(seq, dtype=np.float32)[:, None] * inv_freq[None, :].astype(np.float32)
    cos, sin = np.cos(ang).astype(np.float32), np.sin(ang).astype(np.float32)
    d = np.arange(LANES) % HEAD_DIM
    f = d % ROPE_HALF
    cos_t = np.where(d < ROPE_DIMS, cos[:, f], np.float32(1.0))
    sin_up = np.where((d >= ROPE_HALF) & (d < ROPE_DIMS), sin[:, f], np.float32(0.0))
    sin_dn = np.where(d < ROPE_HALF, -sin[:, f], np.float32(0.0))
    return tuple(jnp.asarray(t, dtype=F32) for t in (cos_t, sin_up, sin_dn))


def kernel(x, mem, norm_g, w_in, mem_norm_g, w_mem_kv, w_out, final_norm_g):
    batch, seq, d_model = x.shape
    assert d_model == D_MODEL and seq % PROJ_ROWS == 0 and seq % (2 * KEY_BLOCK) == 0 and Q_ROWS == KEY_BLOCK
    assert seq // MOBA_BLOCK <= LANES
    rows = batch * seq
    rope = _rope_tables(seq)
    mk, mv = _mem_kv(mem, mem_norm_g.reshape(DEPTH, 1, D_MODEL), w_mem_kv)

    x2d = x.reshape(rows, D_MODEL)
    proj = _first_projection(x2d, norm_g[0:1], w_in, rope, seq)
    for layer in range(DEPTH):
        p = dict(zip(_PROJ_OUT_NAMES, proj))
        seq3 = lambda a: a.reshape(batch, seq, a.shape[-1])
        gates = seq3(p["gates"])
        kmean = p["kmean"].reshape(batch, seq // MOBA_BLOCK, MOBA_WIDTH)
        o_sb, o_mb = _sb_moba_attention([seq3(p[name]) for name in ("sb_q", "sb_k", "sb_v")],
                                        [seq3(p[name]) for name in ("mb_q", "mb_k", "mb_v")], kmean, gates)
        mix = [x2d, o_sb.reshape(rows, SB_WIDTH), o_mb.reshape(rows, MOBA_WIDTH), p["mem_q"], mk, mv, p["gates"],
               w_out]
        if layer + 1 < DEPTH:
            x2d, *proj = _next_projection(mix, layer, norm_g[layer + 1:layer + 2], w_in, rope, seq)
        else:
            out = _final_projection(mix, layer, final_norm_g.reshape(1, D_MODEL), seq)
    return out.reshape(batch, seq, D_MODEL)
```

```python
import functools

import numpy as np
import jax
import jax.numpy as jnp
from jax import lax
from jax.experimental import pallas as pl
from jax.experimental.pallas import tpu as pltpu

D_MODEL = 1024
DEPTH = 2
HEAD_DIM = 64
MEM_HEADS = 4
SB_HEADS = 6
MOBA_HEADS = 6
MEM_WIDTH = MEM_HEADS * HEAD_DIM
SB_WIDTH = SB_HEADS * HEAD_DIM
MOBA_WIDTH = MOBA_HEADS * HEAD_DIM
MIX_WIDTH = SB_WIDTH + MOBA_WIDTH + MEM_WIDTH
IN_COLS = 4 * SB_WIDTH + 4 * MOBA_WIDTH + 2 * MEM_WIDTH
MOBA_BLOCK = 256
MOBA_TOPK = 3
ROPE_THETA = 500000.0
ROPE_DIMS = HEAD_DIM // 4
ROPE_HALF = ROPE_DIMS // 2
NORM_EPS = 1e-6
NEG_INF = -1e30
QK_SCALE = HEAD_DIM ** -0.5
LOG2_E = 1.4426950408889634

LANES = 128
V7X_VMEM_LIMIT = 56 * 1024 * 1024

PAIR = LANES
PROJ_ROWS = 512
PROJ_CHUNK = 512
Q_ROWS = 256
KEY_BLOCK = 256
CUM_BLOCK = 256

F32 = jnp.float32
BF16 = jnp.bfloat16
_NT = (((1,), (1,)), ((), ()))

_GROUPS = (("sb_q", SB_WIDTH), ("sb_k", SB_WIDTH), ("sb_v", SB_WIDTH), ("sb_g", SB_WIDTH),
           ("mb_q", MOBA_WIDTH), ("mb_k", MOBA_WIDTH), ("mb_v", MOBA_WIDTH), ("mb_g", MOBA_WIDTH),
           ("mem_q", MEM_WIDTH), ("mem_g", MEM_WIDTH))
_GATE_OFFSET = {"sb_g": 0, "mb_g": SB_WIDTH, "mem_g": SB_WIDTH + MOBA_WIDTH}


def _unit_table():
    units = []
    for name, width in _GROUPS:
        for off in range(0, width, LANES):
            units.append((name, off))
    return units


_UNITS = _unit_table()


def _rms_norm(x, gain):
    ms = jnp.mean(x * x, axis=-1, keepdims=True)
    return x * lax.rsqrt(ms + NORM_EPS) * gain


def _rotary(y, cos, sin_up, sin_dn):
    return y * cos + pltpu.roll(y, ROPE_HALF, 1) * sin_up + pltpu.roll(y, LANES - ROPE_HALF, 1) * sin_dn


def _project(x, gain_ref, w_ref, cos_ref, sup_ref, sdn_ref, outs):
    h = _rms_norm(x, gain_ref[...]).astype(BF16)
    cos, sup, sdn = cos_ref[...], sup_ref[...], sdn_ref[...]
    units_per_chunk = PROJ_CHUNK // LANES
    for c in range(IN_COLS // PROJ_CHUNK):
        y = jnp.dot(h, w_ref[:, c * PROJ_CHUNK:(c + 1) * PROJ_CHUNK], preferred_element_type=F32)
        for u in range(units_per_chunk):
            name, off = _UNITS[c * units_per_chunk + u]
            piece = y[:, u * LANES:(u + 1) * LANES]
            if name in _GATE_OFFSET:
                g0 = _GATE_OFFSET[name] + off
                outs["gates"][:, g0:g0 + LANES] = piece
                continue
            if name in ("mb_q", "mb_k"):
                piece = _rotary(piece, cos, sup, sdn)
            if name == "mb_k":
                for blk in range(PROJ_ROWS // MOBA_BLOCK):
                    rows = piece[blk * MOBA_BLOCK:(blk + 1) * MOBA_BLOCK]
                    outs["kmean"][0, blk:blk + 1, off:off + LANES] = jnp.mean(rows, axis=0, keepdims=True)
            if name.endswith("_q"):
                piece = piece * (QK_SCALE * LOG2_E)
            if name == "sb_k":
                outs[name][off:off + LANES, :] = piece.T.astype(BF16)
                continue
            outs[name][:, off:off + LANES] = piece.astype(BF16)


_PROJ_OUT_NAMES = ("sb_q", "sb_k", "sb_v", "mb_q", "mb_k", "mb_v", "mem_q", "gates", "kmean")


def _cast_weight_once(w32_ref, w_ref):
    @pl.when(pl.program_id(0) == 0)
    def _():
        cols = w_ref.shape[1]
        chunk = min(cols, PROJ_CHUNK)
        for c in range(cols // chunk):
            w_ref[:, c * chunk:(c + 1) * chunk] = w32_ref[0, :, c * chunk:(c + 1) * chunk].astype(BF16)


def _first_proj_kernel(x_ref, gain_ref, w32_ref, cos_ref, sup_ref, sdn_ref, *rest):
    out_refs, (w_ref,) = rest[:len(_PROJ_OUT_NAMES)], rest[len(_PROJ_OUT_NAMES):]
    _cast_weight_once(w32_ref, w_ref)
    outs = dict(zip(_PROJ_OUT_NAMES, out_refs))
    _project(x_ref[...], gain_ref, w_ref, cos_ref, sup_ref, sdn_ref, outs)


def _memory_attention(q_ref, mk_ref, mv_ref, g_ref):
    outs = []
    for p in range(MEM_WIDTH // PAIR):
        cols = slice(p * PAIR, (p + 1) * PAIR)
        mv = mv_ref[0, 0, :, cols]
        v_ones = jnp.concatenate([mv, jnp.ones_like(mv)], axis=1)
        s = lax.dot_general(_stack_heads(q_ref[:, cols]), mk_ref[0, 0, :, cols], _NT, preferred_element_type=F32)
        outs.append((_unstack_heads(_softmax_pv([s], v_ones)) * _silu(g_ref[:, cols])).astype(BF16))
    return jnp.concatenate(outs, axis=1)


def _residual(x_ref, osb_ref, omb_ref, memq_ref, mk_ref, mv_ref, memg_ref, wo_ref):
    o_mem = _memory_attention(memq_ref, mk_ref, mv_ref, memg_ref)
    mixed = jnp.concatenate([osb_ref[...], omb_ref[...], o_mem], axis=1)
    return x_ref[...] + jnp.dot(mixed, wo_ref[...], preferred_element_type=F32)


_MIX_REFS = 8


def _next_proj_kernel(*refs):
    mix, (gain_ref, w32_ref, cos_ref, sup_ref, sdn_ref, xnew_ref) = refs[:_MIX_REFS], refs[_MIX_REFS:_MIX_REFS + 6]
    out_refs, (w_ref, wo_ref) = refs[_MIX_REFS + 6:-2], refs[-2:]
    _cast_weight_once(w32_ref, w_ref)
    _cast_weight_once(mix[-1], wo_ref)
    outs = dict(zip(_PROJ_OUT_NAMES, out_refs))
    xnew = _residual(*mix[:-1], wo_ref)
    xnew_ref[...] = xnew
    _project(xnew, gain_ref, w_ref, cos_ref, sup_ref, sdn_ref, outs)


def _final_kernel(*refs):
    mix, (gain_ref, out_ref, wo_ref) = refs[:_MIX_REFS], refs[_MIX_REFS:]
    _cast_weight_once(mix[-1], wo_ref)
    out_ref[...] = _rms_norm(_residual(*mix[:-1], wo_ref), gain_ref[...])


def _row_spec(cols):
    return pl.BlockSpec((PROJ_ROWS, cols), lambda r: (r, 0))


def _const_spec(shape):
    return pl.BlockSpec(shape, lambda r: (0,) * len(shape))


def _layer_spec(layer, shape):
    return pl.BlockSpec((1,) + shape, lambda r: (layer,) + (0,) * len(shape))


def _proj_out(rows):
    shapes = {"sb_q": SB_WIDTH, "sb_k": SB_WIDTH, "sb_v": SB_WIDTH, "mb_q": MOBA_WIDTH, "mb_k": MOBA_WIDTH,
              "mb_v": MOBA_WIDTH, "mem_q": MEM_WIDTH}
    out_shape, out_specs = [], []
    for name in _PROJ_OUT_NAMES:
        if name == "gates":
            out_shape.append(jax.ShapeDtypeStruct((rows, MIX_WIDTH), F32))
            out_specs.append(_row_spec(MIX_WIDTH))
        elif name == "kmean":
            blocks = PROJ_ROWS // MOBA_BLOCK
            out_shape.append(jax.ShapeDtypeStruct((rows // PROJ_ROWS, blocks, MOBA_WIDTH), F32))
            out_specs.append(pl.BlockSpec((1, blocks, MOBA_WIDTH), lambda r: (r, 0, 0)))
        elif name == "sb_k":
            out_shape.append(jax.ShapeDtypeStruct((SB_WIDTH, rows), BF16))
            out_specs.append(pl.BlockSpec((SB_WIDTH, PROJ_ROWS), lambda r: (0, r)))
        else:
            out_shape.append(jax.ShapeDtypeStruct((rows, shapes[name]), BF16))
            out_specs.append(_row_spec(shapes[name]))
    return out_shape, out_specs


def _rope_specs(seq):
    steps = seq // PROJ_ROWS
    return [pl.BlockSpec((PROJ_ROWS, LANES), lambda r: (r % steps, 0))] * 3


_PARAMS_1D = pltpu.CompilerParams(dimension_semantics=("arbitrary",), vmem_limit_bytes=V7X_VMEM_LIMIT)
_W_IN_SCRATCH = pltpu.VMEM((D_MODEL, IN_COLS), BF16)
_W_OUT_SCRATCH = pltpu.VMEM((MIX_WIDTH, D_MODEL), BF16)


def _first_projection(x2d, gain, w, rope, seq):
    rows = x2d.shape[0]
    out_shape, out_specs = _proj_out(rows)
    return pl.pallas_call(
        _first_proj_kernel,
        grid=(rows // PROJ_ROWS,),
        in_specs=[_row_spec(D_MODEL), _const_spec((1, D_MODEL)), _layer_spec(0, (D_MODEL, IN_COLS))]
        + _rope_specs(seq),
        out_specs=out_specs,
        out_shape=out_shape,
        scratch_shapes=[_W_IN_SCRATCH],
        compiler_params=_PARAMS_1D,
        name="first_projection",
    )(x2d, gain, w, *rope)


def _mix_specs(layer, seq, mem_len):
    steps_per_batch = seq // PROJ_ROWS
    mem_spec = pl.BlockSpec((1, 1, mem_len, MEM_WIDTH), lambda r: (layer, r // steps_per_batch, 0, 0))
    gate_spec = pl.BlockSpec((PROJ_ROWS, MEM_WIDTH), lambda r: (r, _GATE_OFFSET["mem_g"] // MEM_WIDTH))
    return [_row_spec(D_MODEL), _row_spec(SB_WIDTH), _row_spec(MOBA_WIDTH), _row_spec(MEM_WIDTH),
            mem_spec, mem_spec, gate_spec, _layer_spec(layer, (MIX_WIDTH, D_MODEL))]


def _next_projection(mix, layer, gain, w, rope, seq):
    rows = mix[0].shape[0]
    out_shape, out_specs = _proj_out(rows)
    return pl.pallas_call(
        _next_proj_kernel,
        grid=(rows // PROJ_ROWS,),
        in_specs=_mix_specs(layer, seq, mix[4].shape[2])
        + [_const_spec((1, D_MODEL)), _layer_spec(layer + 1, (D_MODEL, IN_COLS))] + _rope_specs(seq),
        out_specs=[_row_spec(D_MODEL)] + out_specs,
        out_shape=[jax.ShapeDtypeStruct((rows, D_MODEL), F32)] + out_shape,
        scratch_shapes=[_W_IN_SCRATCH, _W_OUT_SCRATCH],
        compiler_params=_PARAMS_1D,
        name="next_projection",
    )(*mix, gain, w, *rope)


def _final_projection(mix, layer, gain, seq):
    rows = mix[0].shape[0]
    return pl.pallas_call(
        _final_kernel,
        grid=(rows // PROJ_ROWS,),
        in_specs=_mix_specs(layer, seq, mix[4].shape[2]) + [_const_spec((1, D_MODEL))],
        out_specs=_row_spec(D_MODEL),
        out_shape=jax.ShapeDtypeStruct((rows, D_MODEL), F32),
        scratch_shapes=[_W_OUT_SCRATCH],
        compiler_params=_PARAMS_1D,
        name="final_projection",
    )(*mix, gain)


def _mem_kv_kernel(mem_ref, gain_ref, w_ref, mk_ref, mv_ref):
    batch, mem_len, _ = mem_ref.shape
    m = _rms_norm(mem_ref[...].reshape(batch * mem_len, D_MODEL), gain_ref[0]).astype(BF16)
    kv = jnp.dot(m, w_ref[0].astype(BF16), preferred_element_type=F32)
    mk_ref[0] =kv[:, :MEM_WIDTH].astype(BF16).reshape(batch, mem_len, MEM_WIDTH)
    mv_ref[0] = kv[:, MEM_WIDTH:].astype(BF16).reshape(batch, mem_len, MEM_WIDTH)


def _mem_kv(mem, gains, w):
    batch, mem_len, _ = mem.shape
    out = jax.ShapeDtypeStruct((DEPTH, batch, mem_len, MEM_WIDTH), BF16)
    out_spec = pl.BlockSpec((1, batch, mem_len, MEM_WIDTH), lambda l: (l, 0, 0, 0))
    return pl.pallas_call(
        _mem_kv_kernel,
        grid=(DEPTH,),
        in_specs=[pl.BlockSpec((batch, mem_len, D_MODEL), lambda l: (0, 0, 0)),
                  pl.BlockSpec((1, 1, D_MODEL), lambda l: (l, 0, 0)),
                  pl.BlockSpec((1, D_MODEL, 2 * MEM_WIDTH), lambda l: (l, 0, 0))],
        out_specs=[out_spec, out_spec],
        out_shape=[out, out],
        compiler_params=_PARAMS_1D,
        name="memory_kv",
    )(mem, gains, w)


def _stack_heads(q_pair):
    lane = lax.broadcasted_iota(jnp.int32, q_pair.shape, 1)
    zero = jnp.zeros_like(q_pair)
    return jnp.concatenate([jnp.where(lane < HEAD_DIM, q_pair, zero), jnp.where(lane >= HEAD_DIM, q_pair, zero)],
                           axis=0)


def _unstack_heads(stacked):
    rows = stacked.shape[0] // 2
    lane = lax.broadcasted_iota(jnp.int32, (rows, stacked.shape[1]), 1)
    return jnp.where(lane < HEAD_DIM, stacked[:rows], stacked[rows:])


def _silu(g):
    return g * jax.nn.sigmoid(g)


def _local_positions(keys):
    row = lax.broadcasted_iota(jnp.int32, (2 * Q_ROWS, keys), 0) % Q_ROWS
    col = lax.broadcasted_iota(jnp.int32, (2 * Q_ROWS, keys), 1)
    return row, col


def _softplus2(z2):
    return jnp.maximum(z2, 0.0) + jnp.log2(1.0 + jnp.exp2(-jnp.abs(z2)))


def _sb_prepare(kt_ref, v_ref, krev_ref, vrev_ref):
    seq = v_ref.shape[1]
    for c in range(seq // CUM_BLOCK):
        src = slice(c * CUM_BLOCK, (c + 1) * CUM_BLOCK)
        dst = slice(seq - (c + 1) * CUM_BLOCK, seq - c * CUM_BLOCK)
        krev_ref[:, dst] = kt_ref[:, src]
        vrev_ref[dst, :] = v_ref[0, src, :]


def _sb_block_pair(m, q_ref, g_ref, cum_ref, o_ref, krev_ref, vrev_ref):
    row, col = _local_positions(CUM_BLOCK)
    past = col < row
    seq = vrev_ref.shape[0]
    half = 2 * Q_ROWS
    first, second = 2 * m, 2 * m + 1
    block_rows = [slice(n * Q_ROWS, (n + 1) * Q_ROWS) for n in (first, second)]
    q = [_stack_heads(q_ref[0, r, :]) for r in block_rows]
    shared = slice(seq - (first + 1) * KEY_BLOCK, seq)
    own = slice(seq - (second + 1) * KEY_BLOCK, seq - second * KEY_BLOCK)
    chunks = (first + 1) * KEY_BLOCK // CUM_BLOCK
    z_own = jnp.dot(q[1], krev_ref[:, own], preferred_element_type=F32)
    z = jnp.dot(jnp.concatenate(q, axis=0), krev_ref[:, shared], preferred_element_type=F32)

    def masked(x, c, block):
        return jnp.where(past, x, 0.0) if (c == 0 and block == 0) else x

    soft_own = jnp.where(past, _softplus2(z_own), 0.0)
    pieces = [soft_own.astype(BF16)]
    tails = [None, jnp.sum(soft_own, axis=-1, keepdims=True)]
    totals = []
    for c in range(chunks):
        soft = [masked(_softplus2(z[b * half:(b + 1) * half, c * CUM_BLOCK:(c + 1) * CUM_BLOCK]), c, b)
                for b in range(2)]
        pieces += [s.astype(BF16) for s in soft]
        totals.append([jnp.sum(s, axis=-1, keepdims=True) for s in soft])
    sums = jnp.dot(jnp.concatenate(pieces, axis=0), cum_ref[...], preferred_element_type=F32)
    w_own = jnp.where(past, jnp.exp2(z_own + sums[:half]), 0.0).astype(BF16)
    weights = []
    for c in range(chunks):
        w = []
        for b in range(2):
            e = (z[b * half:(b + 1) * half, c * CUM_BLOCK:(c + 1) * CUM_BLOCK]
                 + sums[half * (1 + 2 * c + b):half * (2 + 2 * c + b)])
            if tails[b] is not None:
                e = e - tails[b]
            w.append(masked(jnp.exp2(e), c, b).astype(BF16))
            tails[b] = totals[c][b] if tails[b] is None else tails[b] + totals[c][b]
        weights.append(jnp.concatenate(w, axis=0))
    acc = jnp.dot(jnp.concatenate(weights, axis=1), vrev_ref[shared, :], preferred_element_type=F32)
    acc_own = jnp.dot(w_own, vrev_ref[own, :], preferred_element_type=F32)
    outs = [acc[:half], acc[half:] + acc_own]
    for r, out in zip(block_rows, outs):
        o_ref[0, r, :] = (_unstack_heads(out) * _silu(g_ref[0, r, :])).astype(BF16)


def _softmax_pv(blocks, v_ones):
    m = _row_max(blocks)
    p = jnp.concatenate([jnp.exp2(b - m).astype(BF16) for b in blocks], axis=1)
    acc = jnp.dot(p, v_ones, preferred_element_type=F32)
    return acc[:, :PAIR] / acc[:, PAIR:]


def _unselected(q, kmean, n):
    blocks = kmean.shape[0]
    km_hi = kmean.astype(BF16).astype(F32)
    pieces = jnp.concatenate([km_hi, kmean - km_hi], axis=0).astype(BF16)
    g2 = lax.dot_general(pieces, q, _NT, preferred_element_type=F32)
    gate = g2[:blocks] + g2[blocks:]
    blk = lax.broadcasted_iota(jnp.int32, gate.shape, 0)
    gate = jnp.where(blk < n, gate, -jnp.inf)
    ahead = jnp.zeros_like(gate)
    for j in range(n):
        gj = gate[j:j + 1, :]
        ahead = ahead + jnp.where((gj > gate) | ((gj == gate) & (blk > j)), 1.0, 0.0)
    unsel = jnp.where((ahead >= MOBA_TOPK) & (blk < n), 1.0, 0.0)
    unsel = jnp.concatenate([unsel, jnp.zeros((LANES - blocks, unsel.shape[1]), F32)], axis=0)
    return unsel.T.astype(BF16)


def _moba_prepare(k_ref, v_ref, ind_ref, kaug_ref, vaug_ref):
    kaug_ref[:, :PAIR] = k_ref[0]
    kaug_ref[:, PAIR:] = ind_ref[...]
    vaug_ref[:, :PAIR] = v_ref[0]
    vaug_ref[:, PAIR:] = jnp.ones(v_ref.shape[1:], BF16)


def _row_max(tiles):
    m = functools.reduce(jnp.maximum, tiles)
    return jnp.max(jnp.maximum(m[:, :LANES], m[:, LANES:]), axis=-1, keepdims=True)


def _moba_block_pair(m, q_ref, k_ref, kmean_ref, g_ref, o_ref, kaug_ref, vaug_ref):
    row, col = _local_positions(KEY_BLOCK)
    causal = col <= row
    half = 2 * Q_ROWS
    first, second = 2 * m, 2 * m + 1
    block_rows = [slice(n * Q_ROWS, (n + 1) * Q_ROWS) for n in (first, second)]
    assert (MOBA_TOPK + 1) % 2 == 0, "a block pair must not straddle the all-selected threshold"
    gated = first > MOBA_TOPK
    lhs = []
    for n, r in zip((first, second), block_rows):
        q = _stack_heads(q_ref[0, r, :])
        lhs.append(jnp.concatenate([q, _unselected(q, kmean_ref[0], n)], axis=1) if gated else q)

    def keys(lo, hi):
        return kaug_ref[lo:hi, :] if gated else k_ref[0, lo:hi, :]

    shared = (first + 1) * KEY_BLOCK
    own = slice(second * KEY_BLOCK, (second + 1) * KEY_BLOCK)
    s = lax.dot_general(jnp.concatenate(lhs, axis=0), keys(0, shared), _NT, preferred_element_type=F32)
    s_own = jnp.where(causal, lax.dot_general(lhs[1], keys(own.start, own.stop), _NT,
                                              preferred_element_type=F32), NEG_INF)
    tiles = [[s[b * half:(b + 1) * half, j * KEY_BLOCK:(j + 1) * KEY_BLOCK] for j in range(first + 1)]
             for b in range(2)]
    tiles[0][first] = jnp.where(causal, tiles[0][first], NEG_INF)
    maxes = [_row_max(tiles[0]), _row_max(tiles[1] + [s_own])]
    p = jnp.concatenate([jnp.concatenate([jnp.exp2(tiles[b][j] - maxes[b]).astype(BF16) for b in range(2)], axis=0)
                         for j in range(first + 1)], axis=1)
    acc = jnp.dot(p, vaug_ref[:shared, :], preferred_element_type=F32)
    acc_own = jnp.dot(jnp.exp2(s_own - maxes[1]).astype(BF16), vaug_ref[own, :], preferred_element_type=F32)
    for r, a in zip(block_rows, (acc[:half], acc[half:] + acc_own)):
        out = _unstack_heads(a[:, :PAIR] / a[:, PAIR:])
        o_ref[0, r, :] = (out * _silu(g_ref[0, r, :])).astype(BF16)


def _sb_moba_kernel(sq_ref, sk_ref, sv_ref, sg_ref, cum_ref, mq_ref, mk_ref, mv_ref, kmean_ref, ind_ref, mg_ref,
                    so_ref, mo_ref, krev_ref, vrev_ref, kaug_ref, vaug_ref):
    _sb_prepare(sk_ref, sv_ref, krev_ref, vrev_ref)
    _moba_prepare(mk_ref, mv_ref, ind_ref, kaug_ref, vaug_ref)
    n_blocks = sk_ref.shape[1] // KEY_BLOCK
    for m in reversed(range(n_blocks // 2)):
        _sb_block_pair(m, sq_ref, sg_ref, cum_ref, so_ref, krev_ref, vrev_ref)
        _moba_block_pair(m, mq_ref, mk_ref, kmean_ref, mg_ref, mo_ref, kaug_ref, vaug_ref)


_PARAMS_3D = pltpu.CompilerParams(dimension_semantics=("arbitrary", "arbitrary", "arbitrary"),
                                  vmem_limit_bytes=V7X_VMEM_LIMIT)


def _seq_spec(seq, col_block_offset=0):
    return pl.BlockSpec((1, seq, PAIR), lambda b, p, i: (b, 0, p + col_block_offset))


def _neg_cum_matrix():
    j = np.arange(CUM_BLOCK)
    return jnp.asarray(-(j[:, None] >= j[None, :]).astype(np.float32), dtype=BF16)


def _block_indicator(seq):
    s = np.arange(seq)[:, None] // MOBA_BLOCK
    return jnp.asarray(np.where(s == np.arange(LANES)[None, :], NEG_INF, 0.0), dtype=BF16)


def _sb_moba_attention(sb_qkv, mb_qkv, kmean, gates):
    batch, seq, width = sb_qkv[0].shape
    assert SB_WIDTH == MOBA_WIDTH == width
    qkv_specs = [_seq_spec(seq)] * 3
    sb_specs = [_seq_spec(seq), pl.BlockSpec((PAIR, seq), lambda b, p, i: (p, b)), _seq_spec(seq)]
    out = jax.ShapeDtypeStruct((batch, seq, width), BF16)
    return pl.pallas_call(
        _sb_moba_kernel,
        grid=(batch, width // PAIR, 1),
        in_specs=sb_specs + [_seq_spec(seq, _GATE_OFFSET["sb_g"] // PAIR),
                              pl.BlockSpec((CUM_BLOCK, CUM_BLOCK), lambda b, p, i: (0, 0))]
        + qkv_specs + [pl.BlockSpec((1, seq // MOBA_BLOCK, PAIR), lambda b, p, i: (b, 0, p)),
                       pl.BlockSpec((seq, LANES), lambda b, p, i: (0, 0)),
                       _seq_spec(seq, _GATE_OFFSET["mb_g"] // PAIR)],
        out_specs=[_seq_spec(seq), _seq_spec(seq)],
        out_shape=[out, out],
        scratch_shapes=[pltpu.VMEM((PAIR, seq), BF16), pltpu.VMEM((seq, PAIR), BF16),
                        pltpu.VMEM((seq, 2 * PAIR), BF16), pltpu.VMEM((seq, 2 * PAIR), BF16)],
        compiler_params=_PARAMS_3D,
        name="sb_moba_attention",
    )(*sb_qkv, gates, _neg_cum_matrix(), *mb_qkv, kmean, _block_indicator(seq), gates)


def _rope_tables(seq):
    inv_freq = np.float32(ROPE_THETA) ** (-np.arange(ROPE_HALF, dtype=np.float32) * np.float32(2.0 / ROPE_DIMS))
    ang = np.arange(seq, dtype=np.float32)[:, None] * inv_freq[None, :].astype(np.float32)
    cos, sin = np.cos(ang).astype(np.float32), np.sin(ang).astype(np.float32)
    d = np.arange(LANES) % HEAD_DIM
    f = d % ROPE_HALF
    cos_t = np.where(d < ROPE_DIMS, cos[:, f], np.float32(1.0))
    sin_up = np.where((d >= ROPE_HALF) & (d < ROPE_DIMS), sin[:, f], np.float32(0.0))
    sin_dn = np.where(d < ROPE_HALF, -sin[:, f], np.float32(0.0))
    return tuple(jnp.asarray(t, dtype=F32) for t in (cos_t, sin_up, sin_dn))


def kernel(x, mem, norm_g, w_in, mem_norm_g, w_mem_kv, w_out, final_norm_g):
    batch, seq, d_model = x.shape
    assert d_model == D_MODEL and seq % PROJ_ROWS == 0 and seq % (2 * KEY_BLOCK) == 0 and Q_ROWS == KEY_BLOCK
    assert seq // MOBA_BLOCK <= LANES
    rows = batch * seq
    rope = _rope_tables(seq)
    mk, mv = _mem_kv(mem, mem_norm_g.reshape(DEPTH, 1, D_MODEL), w_mem_kv)

    x2d = x.reshape(rows, D_MODEL)
    proj = _first_projection(x2d, norm_g[0:1], w_in, rope, seq)
    for layer in range(DEPTH):
        p = dict(zip(_PROJ_OUT_NAMES, proj))
        seq3 = lambda a: a.reshape(batch, seq, a.shape[-1])
        gates = seq3(p["gates"])
        kmean = p["kmean"].reshape(batch, seq // MOBA_BLOCK, MOBA_WIDTH)
        o_sb, o_mb = _sb_moba_attention([seq3(p["sb_q"]), p["sb_k"], seq3(p["sb_v"])],
                                        [seq3(p[name]) for name in ("mb_q", "mb_k", "mb_v")], kmean, gates)
        mix = [x2d, o_sb.reshape(rows, SB_WIDTH), o_mb.reshape(rows, MOBA_WIDTH), p["mem_q"], mk, mv, p["gates"],
               w_out]
        if layer + 1 < DEPTH:
            x2d, *proj = _next_projection(mix, layer, norm_g[layer + 1:layer + 2], w_in, rope, seq)
        else:
            out = _final_projection(mix, layer, final_norm_g.reshape(1, D_MODEL), seq)
    return out.reshape(batch, seq, D_MODEL)
```

```python
import functools

import numpy as np
import jax
import jax.numpy as jnp
from jax import lax
from jax.experimental import pallas as pl
from jax.experimental.pallas import tpu as pltpu

D_MODEL = 1024
DEPTH = 2
HEAD_DIM = 64
MEM_HEADS = 4
SB_HEADS = 6
MOBA_HEADS = 6
MEM_WIDTH = MEM_HEADS * HEAD_DIM
SB_WIDTH = SB_HEADS * HEAD_DIM
MOBA_WIDTH = MOBA_HEADS * HEAD_DIM
MIX_WIDTH = SB_WIDTH + MOBA_WIDTH + MEM_WIDTH
IN_COLS = 4 * SB_WIDTH + 4 * MOBA_WIDTH + 2 * MEM_WIDTH
MOBA_BLOCK = 256
MOBA_TOPK = 3
ROPE_THETA = 500000.0
ROPE_DIMS = HEAD_DIM // 4
ROPE_HALF = ROPE_DIMS // 2
NORM_EPS = 1e-6
NEG_INF = -1e30
QK_SCALE = HEAD_DIM ** -0.5
LOG2_E = 1.4426950408889634

LANES = 128
V7X_VMEM_LIMIT = 56 * 1024 * 1024

PAIR = LANES
PROJ_ROWS = 512
PROJ_CHUNK = 512
Q_ROWS = 256
KEY_BLOCK = 256
CUM_BLOCK = 256

F32 = jnp.float32
BF16 = jnp.bfloat16
_NT = (((1,), (1,)), ((), ()))

_GROUPS = (("sb_q", SB_WIDTH), ("sb_k", SB_WIDTH), ("sb_v", SB_WIDTH), ("sb_g", SB_WIDTH),
           ("mb_q", MOBA_WIDTH), ("mb_k", MOBA_WIDTH), ("mb_v", MOBA_WIDTH), ("mb_g", MOBA_WIDTH),
           ("mem_q", MEM_WIDTH), ("mem_g", MEM_WIDTH))
_GATE_OFFSET = {"sb_g": 0, "mb_g": SB_WIDTH, "mem_g": SB_WIDTH + MOBA_WIDTH}


def _unit_table():
    units = []
    for name, width in _GROUPS:
        for off in range(0, width, LANES):
            units.append((name, off))
    return units


_UNITS = _unit_table()


def _rms_norm(x, gain):
    ms = jnp.mean(x * x, axis=-1, keepdims=True)
    return x * lax.rsqrt(ms + NORM_EPS) * gain


def _rotary(y, cos, sin_up, sin_dn):
    return y * cos + pltpu.roll(y, ROPE_HALF, 1) * sin_up + pltpu.roll(y, LANES - ROPE_HALF, 1) * sin_dn


def _project(x, gain_ref, w_ref, cos_ref, sup_ref, sdn_ref, outs):
    h = _rms_norm(x, gain_ref[...]).astype(BF16)
    cos, sup, sdn = cos_ref[...], sup_ref[...], sdn_ref[...]
    units_per_chunk = PROJ_CHUNK // LANES
    for c in range(IN_COLS // PROJ_CHUNK):
        y = jnp.dot(h, w_ref[:, c * PROJ_CHUNK:(c + 1) * PROJ_CHUNK], preferred_element_type=F32)
        for u in range(units_per_chunk):
            name, off = _UNITS[c * units_per_chunk + u]
            piece = y[:, u * LANES:(u + 1) * LANES]
            if name in _GATE_OFFSET:
                g0 = _GATE_OFFSET[name] + off
                outs["gates"][:, g0:g0 + LANES] = piece
                continue
            if name in ("mb_q", "mb_k"):
                piece = _rotary(piece, cos, sup, sdn)
            if name == "mb_k":
                for blk in range(PROJ_ROWS // MOBA_BLOCK):
                    rows = piece[blk * MOBA_BLOCK:(blk + 1) * MOBA_BLOCK]
                    outs["kmean"][0, blk:blk + 1, off:off + LANES] = jnp.mean(rows, axis=0, keepdims=True)
            if name.endswith("_q"):
                piece = piece * (QK_SCALE * LOG2_E)
            if name in ("sb_k", "mb_k"):
                outs[name][off:off + LANES, :] = piece.T.astype(BF16)
                continue
            outs[name][:, off:off + LANES] = piece.astype(BF16)


_PROJ_OUT_NAMES = ("sb_q", "sb_k", "sb_v", "mb_q", "mb_k", "mb_v", "mem_q", "gates", "kmean")


def _cast_weight_once(w32_ref, w_ref):
    @pl.when(pl.program_id(0) == 0)
    def _():
        cols = w_ref.shape[1]
        chunk = min(cols, PROJ_CHUNK)
        for c in range(cols // chunk):
            w_ref[:, c * chunk:(c + 1) * chunk] = w32_ref[0, :, c * chunk:(c + 1) * chunk].astype(BF16)


def _first_proj_kernel(x_ref, gain_ref, w32_ref, cos_ref, sup_ref, sdn_ref, *rest):
    out_refs, (w_ref,) = rest[:len(_PROJ_OUT_NAMES)], rest[len(_PROJ_OUT_NAMES):]
    _cast_weight_once(w32_ref, w_ref)
    outs = dict(zip(_PROJ_OUT_NAMES, out_refs))
    _project(x_ref[...], gain_ref, w_ref, cos_ref, sup_ref, sdn_ref, outs)


def _memory_attention(q_ref, mk_ref, mv_ref, g_ref):
    outs = []
    for p in range(MEM_WIDTH // PAIR):
        cols = slice(p * PAIR, (p + 1) * PAIR)
        mv = mv_ref[0, 0, :, cols]
        v_ones = jnp.concatenate([mv, jnp.ones_like(mv)], axis=1)
        s = lax.dot_general(_stack_heads(q_ref[:, cols]), mk_ref[0, 0, :, cols], _NT, preferred_element_type=F32)
        outs.append((_unstack_heads(_softmax_pv([s], v_ones)) * _silu(g_ref[:, cols])).astype(BF16))
    return jnp.concatenate(outs, axis=1)


def _residual(x_ref, osb_ref, omb_ref, memq_ref, mk_ref, mv_ref, memg_ref, wo_ref):
    o_mem = _memory_attention(memq_ref, mk_ref, mv_ref, memg_ref)
    mixed = jnp.concatenate([osb_ref[...], omb_ref[...], o_mem], axis=1)
    return x_ref[...] + jnp.dot(mixed, wo_ref[...], preferred_element_type=F32)


_MIX_REFS = 8


def _next_proj_kernel(*refs):
    mix, (gain_ref, w32_ref, cos_ref, sup_ref, sdn_ref, xnew_ref) = refs[:_MIX_REFS], refs[_MIX_REFS:_MIX_REFS + 6]
    out_refs, (w_ref, wo_ref) = refs[_MIX_REFS + 6:-2], refs[-2:]
    _cast_weight_once(w32_ref, w_ref)
    _cast_weight_once(mix[-1], wo_ref)
    outs = dict(zip(_PROJ_OUT_NAMES, out_refs))
    xnew = _residual(*mix[:-1], wo_ref)
    xnew_ref[...] = xnew
    _project(xnew, gain_ref, w_ref, cos_ref, sup_ref, sdn_ref, outs)


def _final_kernel(*refs):
    mix, (gain_ref, out_ref, wo_ref) = refs[:_MIX_REFS], refs[_MIX_REFS:]
    _cast_weight_once(mix[-1], wo_ref)
    out_ref[...] = _rms_norm(_residual(*mix[:-1], wo_ref), gain_ref[...])


def _row_spec(cols):
    return pl.BlockSpec((PROJ_ROWS, cols), lambda r: (r, 0))


def _const_spec(shape):
    return pl.BlockSpec(shape, lambda r: (0,) * len(shape))


def _layer_spec(layer, shape):
    return pl.BlockSpec((1,) + shape, lambda r: (layer,) + (0,) * len(shape))


def _proj_out(rows):
    shapes = {"sb_q": SB_WIDTH, "sb_k": SB_WIDTH, "sb_v": SB_WIDTH, "mb_q": MOBA_WIDTH, "mb_k": MOBA_WIDTH,
              "mb_v": MOBA_WIDTH, "mem_q": MEM_WIDTH}
    out_shape, out_specs = [], []
    for name in _PROJ_OUT_NAMES:
        if name == "gates":
            out_shape.append(jax.ShapeDtypeStruct((rows, MIX_WIDTH), F32))
            out_specs.append(_row_spec(MIX_WIDTH))
        elif name == "kmean":
            blocks = PROJ_ROWS // MOBA_BLOCK
            out_shape.append(jax.ShapeDtypeStruct((rows // PROJ_ROWS, blocks, MOBA_WIDTH), F32))
            out_specs.append(pl.BlockSpec((1, blocks, MOBA_WIDTH), lambda r: (r, 0, 0)))
        elif name in ("sb_k", "mb_k"):
            out_shape.append(jax.ShapeDtypeStruct((shapes[name], rows), BF16))
            out_specs.append(pl.BlockSpec((shapes[name], PROJ_ROWS), lambda r: (0, r)))
        else:
            out_shape.append(jax.ShapeDtypeStruct((rows, shapes[name]), BF16))
            out_specs.append(_row_spec(shapes[name]))
    return out_shape, out_specs


def _rope_specs(seq):
    steps = seq // PROJ_ROWS
    return [pl.BlockSpec((PROJ_ROWS, LANES), lambda r: (r % steps, 0))] * 3


_PARAMS_1D = pltpu.CompilerParams(dimension_semantics=("arbitrary",), vmem_limit_bytes=V7X_VMEM_LIMIT)
_W_IN_SCRATCH = pltpu.VMEM((D_MODEL, IN_COLS), BF16)
_W_OUT_SCRATCH = pltpu.VMEM((MIX_WIDTH, D_MODEL), BF16)


def _first_projection(x2d, gain, w, rope, seq):
    rows = x2d.shape[0]
    out_shape, out_specs = _proj_out(rows)
    return pl.pallas_call(
        _first_proj_kernel,
        grid=(rows // PROJ_ROWS,),
        in_specs=[_row_spec(D_MODEL), _const_spec((1, D_MODEL)), _layer_spec(0, (D_MODEL, IN_COLS))]
        + _rope_specs(seq),
        out_specs=out_specs,
        out_shape=out_shape,
        scratch_shapes=[_W_IN_SCRATCH],
        compiler_params=_PARAMS_1D,
        name="first_projection",
    )(x2d, gain, w, *rope)


def _mix_specs(layer, seq, mem_len):
    steps_per_batch = seq // PROJ_ROWS
    mem_spec = pl.BlockSpec((1, 1, mem_len, MEM_WIDTH), lambda r: (layer, r // steps_per_batch, 0, 0))
    gate_spec = pl.BlockSpec((PROJ_ROWS, MEM_WIDTH), lambda r: (r, _GATE_OFFSET["mem_g"] // MEM_WIDTH))
    return [_row_spec(D_MODEL), _row_spec(SB_WIDTH), _row_spec(MOBA_WIDTH), _row_spec(MEM_WIDTH),
            mem_spec, mem_spec, gate_spec, _layer_spec(layer, (MIX_WIDTH, D_MODEL))]


def _next_projection(mix, layer, gain, w, rope, seq):
    rows = mix[0].shape[0]
    out_shape, out_specs = _proj_out(rows)
    return pl.pallas_call(
        _next_proj_kernel,
        grid=(rows // PROJ_ROWS,),
        in_specs=_mix_specs(layer, seq, mix[4].shape[2])
        + [_const_spec((1, D_MODEL)), _layer_spec(layer + 1, (D_MODEL, IN_COLS))] + _rope_specs(seq),
        out_specs=[_row_spec(D_MODEL)] + out_specs,
        out_shape=[jax.ShapeDtypeStruct((rows, D_MODEL), F32)] + out_shape,
        scratch_shapes=[_W_IN_SCRATCH, _W_OUT_SCRATCH],
        compiler_params=_PARAMS_1D,
        name="next_projection",
    )(*mix, gain, w, *rope)


def _final_projection(mix, layer, gain, seq):
    rows = mix[0].shape[0]
    return pl.pallas_call(
        _final_kernel,
        grid=(rows // PROJ_ROWS,),
        in_specs=_mix_specs(layer, seq, mix[4].shape[2]) + [_const_spec((1, D_MODEL))],
        out_specs=_row_spec(D_MODEL),
        out_shape=jax.ShapeDtypeStruct((rows, D_MODEL), F32),
        scratch_shapes=[_W_OUT_SCRATCH],
        compiler_params=_PARAMS_1D,
        name="final_projection",
    )(*mix, gain)


def _mem_kv_kernel(mem_ref, gain_ref, w_ref, mk_ref, mv_ref):
    batch, mem_len, _ = mem_ref.shape
    m = _rms_norm(mem_ref[...].reshape(batch * mem_len, D_MODEL), gain_ref[0]).astype(BF16)
    kv = jnp.dot(m, w_ref[0].astype(BF16), preferred_element_type=F32)
    mk_ref[0] =kv[:, :MEM_WIDTH].astype(BF16).reshape(batch, mem_len, MEM_WIDTH)
    mv_ref[0] = kv[:, MEM_WIDTH:].astype(BF16).reshape(batch, mem_len, MEM_WIDTH)


def _mem_kv(mem, gains, w):
    batch, mem_len, _ = mem.shape
    out = jax.ShapeDtypeStruct((DEPTH, batch, mem_len, MEM_WIDTH), BF16)
    out_spec = pl.BlockSpec((1, batch, mem_len, MEM_WIDTH), lambda l: (l, 0, 0, 0))
    return pl.pallas_call(
        _mem_kv_kernel,
        grid=(DEPTH,),
        in_specs=[pl.BlockSpec((batch, mem_len, D_MODEL), lambda l: (0, 0, 0)),
                  pl.BlockSpec((1, 1, D_MODEL), lambda l: (l, 0, 0)),
                  pl.BlockSpec((1, D_MODEL, 2 * MEM_WIDTH), lambda l: (l, 0, 0))],
        out_specs=[out_spec, out_spec],
        out_shape=[out, out],
        compiler_params=_PARAMS_1D,
        name="memory_kv",
    )(mem, gains, w)


def _stack_heads(q_pair):
    lane = lax.broadcasted_iota(jnp.int32, q_pair.shape, 1)
    zero = jnp.zeros_like(q_pair)
    return jnp.concatenate([jnp.where(lane < HEAD_DIM, q_pair, zero), jnp.where(lane >= HEAD_DIM, q_pair, zero)],
                           axis=0)


def _unstack_heads(stacked):
    rows = stacked.shape[0] // 2
    lane = lax.broadcasted_iota(jnp.int32, (rows, stacked.shape[1]), 1)
    return jnp.where(lane < HEAD_DIM, stacked[:rows], stacked[rows:])


def _silu(g):
    return g * jax.nn.sigmoid(g)


def _local_positions(keys):
    row = lax.broadcasted_iota(jnp.int32, (2 * Q_ROWS, keys), 0) % Q_ROWS
    col = lax.broadcasted_iota(jnp.int32, (2 * Q_ROWS, keys), 1)
    return row, col


def _softplus2(z2):
    return jnp.maximum(z2, 0.0) + jnp.log2(1.0 + jnp.exp2(-jnp.abs(z2)))


def _sb_prepare(kt_ref, v_ref, krev_ref, vrev_ref):
    seq = v_ref.shape[1]
    for c in range(seq // CUM_BLOCK):
        src = slice(c * CUM_BLOCK, (c + 1) * CUM_BLOCK)
        dst = slice(seq - (c + 1) * CUM_BLOCK, seq - c * CUM_BLOCK)
        krev_ref[:, dst] = kt_ref[:, src]
        vrev_ref[dst, :] = v_ref[0, src, :]


def _sb_block_pair(m, q_ref, g_ref, cum_ref, o_ref, krev_ref, vrev_ref):
    row, col = _local_positions(CUM_BLOCK)
    past = col < row
    seq = vrev_ref.shape[0]
    half = 2 * Q_ROWS
    first, second = 2 * m, 2 * m + 1
    block_rows = [slice(n * Q_ROWS, (n + 1) * Q_ROWS) for n in (first, second)]
    q = [_stack_heads(q_ref[0, r, :]) for r in block_rows]
    shared = slice(seq - (first + 1) * KEY_BLOCK, seq)
    own = slice(seq - (second + 1) * KEY_BLOCK, seq - second * KEY_BLOCK)
    chunks = (first + 1) * KEY_BLOCK // CUM_BLOCK
    z_own = jnp.dot(q[1], krev_ref[:, own], preferred_element_type=F32)
    z = jnp.dot(jnp.concatenate(q, axis=0), krev_ref[:, shared], preferred_element_type=F32)

    def masked(x, c, block):
        return jnp.where(past, x, 0.0) if (c == 0 and block == 0) else x

    soft_own = jnp.where(past, _softplus2(z_own), 0.0)
    pieces = [soft_own.astype(BF16)]
    tails = [None, jnp.sum(soft_own, axis=-1, keepdims=True)]
    totals = []
    for c in range(chunks):
        soft = [masked(_softplus2(z[b * half:(b + 1) * half, c * CUM_BLOCK:(c + 1) * CUM_BLOCK]), c, b)
                for b in range(2)]
        pieces += [s.astype(BF16) for s in soft]
        totals.append([jnp.sum(s, axis=-1, keepdims=True) for s in soft])
    sums = jnp.dot(jnp.concatenate(pieces, axis=0), cum_ref[...], preferred_element_type=F32)
    w_own = jnp.where(past, jnp.exp2(z_own + sums[:half]), 0.0).astype(BF16)
    weights = []
    for c in range(chunks):
        w = []
        for b in range(2):
            e = (z[b * half:(b + 1) * half, c * CUM_BLOCK:(c + 1) * CUM_BLOCK]
                 + sums[half * (1 + 2 * c + b):half * (2 + 2 * c + b)])
            if tails[b] is not None:
                e = e - tails[b]
            w.append(masked(jnp.exp2(e), c, b).astype(BF16))
            tails[b] = totals[c][b] if tails[b] is None else tails[b] + totals[c][b]
        weights.append(jnp.concatenate(w, axis=0))
    acc = jnp.dot(jnp.concatenate(weights, axis=1), vrev_ref[shared, :], preferred_element_type=F32)
    acc_own = jnp.dot(w_own, vrev_ref[own, :], preferred_element_type=F32)
    outs = [acc[:half], acc[half:] + acc_own]
    for r, out in zip(block_rows, outs):
        o_ref[0, r, :] = (_unstack_heads(out) * _silu(g_ref[0, r, :])).astype(BF16)


def _softmax_pv(blocks, v_ones):
    m = _row_max(blocks)
    p = jnp.concatenate([jnp.exp2(b - m).astype(BF16) for b in blocks], axis=1)
    acc = jnp.dot(p, v_ones, preferred_element_type=F32)
    return acc[:, :PAIR] / acc[:, PAIR:]


def _unselected(q, kmean, n):
    blocks = kmean.shape[0]
    km_hi = kmean.astype(BF16).astype(F32)
    pieces = jnp.concatenate([km_hi, kmean - km_hi], axis=0).astype(BF16)
    g2 = lax.dot_general(pieces, q, _NT, preferred_element_type=F32)
    gate = g2[:blocks] + g2[blocks:]
    blk = lax.broadcasted_iota(jnp.int32, gate.shape, 0)
    gate = jnp.where(blk < n, gate, -jnp.inf)
    ahead = jnp.zeros_like(gate)
    for j in range(n):
        gj = gate[j:j + 1, :]
        ahead = ahead + jnp.where((gj > gate) | ((gj == gate) & (blk > j)), 1.0, 0.0)
    unsel = jnp.where((ahead >= MOBA_TOPK) & (blk < n), 1.0, 0.0)
    unsel = jnp.concatenate([unsel, jnp.zeros((LANES - blocks, unsel.shape[1]), F32)], axis=0)
    return unsel.T.astype(BF16)


def _moba_prepare(kt_ref, v_ref, ind_ref, kaug_ref, vaug_ref):
    kaug_ref[:PAIR, :] = kt_ref[...]
    kaug_ref[PAIR:, :] = ind_ref[...]
    vaug_ref[:, :PAIR] = v_ref[0]
    vaug_ref[:, PAIR:] = jnp.ones(v_ref.shape[1:], BF16)


def _row_max(tiles):
    m = functools.reduce(jnp.maximum, tiles)
    return jnp.max(jnp.maximum(m[:, :LANES], m[:, LANES:]), axis=-1, keepdims=True)


def _moba_block_pair(m, q_ref, k_ref, kmean_ref, g_ref, o_ref, kaug_ref, vaug_ref):
    row, col = _local_positions(KEY_BLOCK)
    causal = col <= row
    half = 2 * Q_ROWS
    first, second = 2 * m, 2 * m + 1
    block_rows = [slice(n * Q_ROWS, (n + 1) * Q_ROWS) for n in (first, second)]
    assert (MOBA_TOPK + 1) % 2 == 0, "a block pair must not straddle the all-selected threshold"
    gated = first > MOBA_TOPK
    lhs = []
    for n, r in zip((first, second), block_rows):
        q = _stack_heads(q_ref[0, r, :])
        lhs.append(jnp.concatenate([q, _unselected(q, kmean_ref[0], n)], axis=1) if gated else q)

    def keys(lo, hi):
        return kaug_ref[:, lo:hi] if gated else k_ref[:, lo:hi]

    shared = (first + 1) * KEY_BLOCK
    own = slice(second * KEY_BLOCK, (second + 1) * KEY_BLOCK)
    s = jnp.dot(jnp.concatenate(lhs, axis=0), keys(0, shared), preferred_element_type=F32)
    s_own = jnp.where(causal, jnp.dot(lhs[1], keys(own.start, own.stop), preferred_element_type=F32), NEG_INF)
    tiles = [[s[b * half:(b + 1) * half, j * KEY_BLOCK:(j + 1) * KEY_BLOCK] for j in range(first + 1)]
             for b in range(2)]
    tiles[0][first] = jnp.where(causal, tiles[0][first], NEG_INF)
    maxes = [_row_max(tiles[0]), _row_max(tiles[1] + [s_own])]
    p = jnp.concatenate([jnp.concatenate([jnp.exp2(tiles[b][j] - maxes[b]).astype(BF16) for b in range(2)], axis=0)
                         for j in range(first + 1)], axis=1)
    acc = jnp.dot(p, vaug_ref[:shared, :], preferred_element_type=F32)
    acc_own = jnp.dot(jnp.exp2(s_own - maxes[1]).astype(BF16), vaug_ref[own, :], preferred_element_type=F32)
    for r, a in zip(block_rows, (acc[:half], acc[half:] + acc_own)):
        out = _unstack_heads(a[:, :PAIR] / a[:, PAIR:])
        o_ref[0, r, :] = (out * _silu(g_ref[0, r, :])).astype(BF16)


def _sb_moba_kernel(sq_ref, sk_ref, sv_ref, sg_ref, cum_ref, mq_ref, mk_ref, mv_ref, kmean_ref, ind_ref, mg_ref,
                    so_ref, mo_ref, krev_ref, vrev_ref, kaug_ref, vaug_ref):
    _sb_prepare(sk_ref, sv_ref, krev_ref, vrev_ref)
    _moba_prepare(mk_ref, mv_ref, ind_ref, kaug_ref, vaug_ref)
    n_blocks = sk_ref.shape[1] // KEY_BLOCK
    for m in reversed(range(n_blocks // 2)):
        _sb_block_pair(m, sq_ref, sg_ref, cum_ref, so_ref, krev_ref, vrev_ref)
        _moba_block_pair(m, mq_ref, mk_ref, kmean_ref, mg_ref, mo_ref, kaug_ref, vaug_ref)


_PARAMS_3D = pltpu.CompilerParams(dimension_semantics=("arbitrary", "arbitrary", "arbitrary"),
                                  vmem_limit_bytes=V7X_VMEM_LIMIT)


def _seq_spec(seq, col_block_offset=0):
    return pl.BlockSpec((1, seq, PAIR), lambda b, p, i: (b, 0, p + col_block_offset))


def _neg_cum_matrix():
    j = np.arange(CUM_BLOCK)
    return jnp.asarray(-(j[:, None] >= j[None, :]).astype(np.float32), dtype=BF16)


def _block_indicator(seq):
    s = np.arange(seq)[None, :] // MOBA_BLOCK
    return jnp.asarray(np.where(s == np.arange(LANES)[:, None], NEG_INF, 0.0), dtype=BF16)


def _sb_moba_attention(sb_qkv, mb_qkv, kmean, gates):
    batch, seq, width = sb_qkv[0].shape
    assert SB_WIDTH == MOBA_WIDTH == width
    sb_specs = [_seq_spec(seq), pl.BlockSpec((PAIR, seq), lambda b, p, i: (p, b)), _seq_spec(seq)]
    out = jax.ShapeDtypeStruct((batch, seq, width), BF16)
    return pl.pallas_call(
        _sb_moba_kernel,
        grid=(batch, width // PAIR, 1),
        in_specs=sb_specs + [_seq_spec(seq, _GATE_OFFSET["sb_g"] // PAIR),
                              pl.BlockSpec((CUM_BLOCK, CUM_BLOCK), lambda b, p, i: (0, 0))]
        + sb_specs + [pl.BlockSpec((1, seq // MOBA_BLOCK, PAIR), lambda b, p, i: (b, 0, p)),
                      pl.BlockSpec((LANES, seq), lambda b, p, i: (0, 0)),
                       _seq_spec(seq, _GATE_OFFSET["mb_g"] // PAIR)],
        out_specs=[_seq_spec(seq), _seq_spec(seq)],
        out_shape=[out, out],
        scratch_shapes=[pltpu.VMEM((PAIR, seq), BF16), pltpu.VMEM((seq, PAIR), BF16),
                        pltpu.VMEM((2 * PAIR, seq), BF16), pltpu.VMEM((seq, 2 * PAIR), BF16)],
        compiler_params=_PARAMS_3D,
        name="sb_moba_attention",
    )(*sb_qkv, gates, _neg_cum_matrix(), *mb_qkv, kmean, _block_indicator(seq), gates)


def _rope_tables(seq):
    inv_freq = np.float32(ROPE_THETA) ** (-np.arange(ROPE_HALF, dtype=np.float32) * np.float32(2.0 / ROPE_DIMS))
    ang = np.arange(seq, dtype=np.float32)[:, None] * inv_freq[None, :].astype(np.float32)
    cos, sin = np.cos(ang).astype(np.float32), np.sin(ang).astype(np.float32)
    d = np.arange(LANES) % HEAD_DIM
    f = d % ROPE_HALF
    cos_t = np.where(d < ROPE_DIMS, cos[:, f], np.float32(1.0))
    sin_up = np.where((d >= ROPE_HALF) & (d < ROPE_DIMS), sin[:, f], np.float32(0.0))
    sin_dn = np.where(d < ROPE_HALF, -sin[:, f], np.float32(0.0))
    return tuple(jnp.asarray(t, dtype=F32) for t in (cos_t, sin_up, sin_dn))


def kernel(x, mem, norm_g, w_in, mem_norm_g, w_mem_kv, w_out, final_norm_g):
    batch, seq, d_model = x.shape
    assert d_model == D_MODEL and seq % PROJ_ROWS == 0 and seq % (2 * KEY_BLOCK) == 0 and Q_ROWS == KEY_BLOCK
    assert seq // MOBA_BLOCK <= LANES
    rows = batch * seq
    rope = _rope_tables(seq)
    mk, mv = _mem_kv(mem, mem_norm_g.reshape(DEPTH, 1, D_MODEL), w_mem_kv)

    x2d = x.reshape(rows, D_MODEL)
    proj = _first_projection(x2d, norm_g[0:1], w_in, rope, seq)
    for layer in range(DEPTH):
        p = dict(zip(_PROJ_OUT_NAMES, proj))
        seq3 = lambda a: a.reshape(batch, seq, a.shape[-1])
        gates = seq3(p["gates"])
        kmean = p["kmean"].reshape(batch, seq // MOBA_BLOCK, MOBA_WIDTH)
        o_sb, o_mb = _sb_moba_attention([seq3(p["sb_q"]), p["sb_k"], seq3(p["sb_v"])],
                                        [seq3(p["mb_q"]), p["mb_k"], seq3(p["mb_v"])], kmean, gates)
        mix = [x2d, o_sb.reshape(rows, SB_WIDTH), o_mb.reshape(rows, MOBA_WIDTH), p["mem_q"], mk, mv, p["gates"],
               w_out]
        if layer + 1 < DEPTH:
            x2d, *proj = _next_projection(mix, layer, norm_g[layer + 1:layer + 2], w_in, rope, seq)
        else:
            out = _final_projection(mix, layer, final_norm_g.reshape(1, D_MODEL), seq)
    return out.reshape(batch, seq, D_MODEL)
```
